```python
import math
import jax, jax.numpy as jnp
from jax import lax
import numpy as np

D_MODEL = 1024
BATCH = 4
SEQ = 4096
DEPTH = 4

HEAD_DIM = 64
REC_WIDTH = 3 * D_MODEL // 8
REC_HEADS = REC_WIDTH // HEAD_DIM
REC_CONV = 4
REC_C = 8.0
DIFF_WIDTH = 3 * D_MODEL // 8
DIFF_HEADS = DIFF_WIDTH // HEAD_DIM
DIFF_SUB = HEAD_DIM // 2
ROPE_THETA = 500000.0
ROPE_DIM = DIFF_SUB // 4
Q_BLOCK = 128
RET_WIDTH = D_MODEL // 4
RET_HEADS = RET_WIDTH // HEAD_DIM
RET_CHUNK = 128
RET_ROT_BASE = 10000.0
MIX_WIDTH = REC_WIDTH + DIFF_WIDTH + RET_WIDTH
IN_SIZES = (REC_WIDTH, REC_WIDTH,
            DIFF_WIDTH, DIFF_WIDTH, DIFF_WIDTH,
            RET_WIDTH, RET_WIDTH, RET_WIDTH, RET_WIDTH)
IN_WIDTH = sum(IN_SIZES)
D_FF = 2816
N_MOD = 9
EPS = 1e-6

kernel_name = "hybrid_macaron_rglru_diffattn_retention_adaln"


def rmsnorm(x, g):
    xf = x.astype(jnp.float32)
    y = xf * lax.rsqrt(jnp.mean(xf * xf, axis=-1, keepdims=True) + EPS)
    return (y * g.astype(jnp.float32)).astype(x.dtype)


def modulate(h, shift, scale):
    return h * (1.0 + scale) + shift


def swiglu(h, w_in, w_out):
    hg, hu = jnp.split(h @ w_in, 2, axis=-1)
    return (jax.nn.silu(hg) * hu) @ w_out


def rotary(x, pos, rot_dim, base):
    half = rot_dim // 2
    inv = jnp.power(jnp.float32(base), -jnp.arange(half, dtype=jnp.float32) * (2.0 / rot_dim))
    ang = pos.astype(jnp.float32)[:, :, None] * inv
    ang = ang.reshape(ang.shape[:2] + (1,) * (x.ndim - 3) + (half,))
    cos, sin = jnp.cos(ang), jnp.sin(ang)
    xr = x[..., :rot_dim].astype(jnp.float32)
    x1, x2 = xr[..., :half], xr[..., half:]
    rot = jnp.concatenate([x1 * cos - x2 * sin, x2 * cos + x1 * sin], axis=-1).astype(x.dtype)
    return jnp.concatenate([rot, x[..., rot_dim:]], axis=-1)


def _lru_combine(left, right):
    a_l, b_l = left
    a_r, b_r = right
    return a_l * a_r, a_r * b_l + b_r


def rglru_group(xr, gate, conv_w, conv_b, wa, ba, wx, bx, lam):
    B, T, _ = xr.shape
    xc = lax.conv_general_dilated(xr, conv_w[:, None, :], window_strides=(1,),
                                  padding=[(REC_CONV - 1, 0)],
                                  dimension_numbers=('NWC', 'WIO', 'NWC'),
                                  feature_group_count=REC_WIDTH) + conv_b
    xb = xc.reshape(B, T, REC_HEADS, HEAD_DIM)
    r = jax.nn.sigmoid(jnp.einsum('bthi,hij->bthj', xb, wa).reshape(B, T, REC_WIDTH) + ba)
    i = jax.nn.sigmoid(jnp.einsum('bthi,hij->bthj', xb, wx).reshape(B, T, REC_WIDTH) + bx)
    log_a = -REC_C * r.astype(jnp.float32) * jax.nn.softplus(-lam.astype(jnp.float32))
    a = jnp.exp(log_a)
    b = jnp.sqrt(-jnp.expm1(2.0 * log_a)) * (i * xc).astype(jnp.float32)
    _, h = lax.associative_scan(_lru_combine, (a, b), axis=1)
    return jax.nn.gelu(gate) * h.astype(gate.dtype)


def diff_attention_group(q, k, v, positions, lq1, lk1, lq2, lk2, subln_g, lambda_init):
    B, T, _ = q.shape
    q = rotary(q.reshape(B, T, DIFF_HEADS, 2, DIFF_SUB), positions, ROPE_DIM, ROPE_THETA)
    k = rotary(k.reshape(B, T, DIFF_HEADS, 2, DIFF_SUB), positions, ROPE_DIM, ROPE_THETA)
    q = q.transpose(0, 2, 3, 1, 4)
    k = k.transpose(0, 2, 3, 1, 4)
    v = v.reshape(B, T, DIFF_HEADS, HEAD_DIM).transpose(0, 2, 1, 3)
    scale = DIFF_SUB ** -0.5
    lam = (jnp.exp(jnp.sum(lq1.astype(jnp.float32) * lk1.astype(jnp.float32)))
           - jnp.exp(jnp.sum(lq2.astype(jnp.float32) * lk2.astype(jnp.float32))) + lambda_init)
    outs = []
    for blk in range(T // Q_BLOCK):
        s0 = blk * Q_BLOCK
        end = s0 + Q_BLOCK
        scores = jnp.einsum('bhsqd,bhskd->bhsqk', q[:, :, :, s0:end], k[:, :, :, :end]).astype(jnp.float32) * scale
        mask = jnp.arange(end)[None, :] <= (s0 + jnp.arange(Q_BLOCK))[:, None]
        p = jax.nn.softmax(jnp.where(mask, scores, -jnp.inf), axis=-1)
        w = p[:, :, 0] - lam * p[:, :, 1]
        outs.append(jnp.einsum('bhqk,bhkd->bhqd', w.astype(v.dtype), v[:, :, :end]))
    o = jnp.concatenate(outs, axis=2)
    o = rmsnorm(o, subln_g) * (1.0 - lambda_init)
    return o.transpose(0, 2, 1, 3).reshape(B, T, DIFF_WIDTH)


def retention_group(q, k, v, g, positions):
    B, T, _ = q.shape
    C = RET_CHUNK
    N = T // C
    q = rotary(q.reshape(B, T, RET_HEADS, HEAD_DIM), positions, HEAD_DIM, RET_ROT_BASE)
    k = rotary(k.reshape(B, T, RET_HEADS, HEAD_DIM), positions, HEAD_DIM, RET_ROT_BASE) * (HEAD_DIM ** -0.5)
    v = v.reshape(B, T, RET_HEADS, HEAD_DIM)
    to_chunks = lambda t: t.astype(jnp.float32).transpose(0, 2, 1, 3).reshape(B, RET_HEADS, N, C, HEAD_DIM)
    qc, kc, vc = to_chunks(q), to_chunks(k), to_chunks(v)
    lg = jnp.log1p(-jnp.exp2(-5.0 - jnp.arange(RET_HEADS, dtype=jnp.float32)))
    idx = jnp.arange(C, dtype=jnp.float32)
    rel = idx[:, None] - idx[None, :]
    decay = jnp.where(rel >= 0, jnp.exp(jnp.maximum(rel, 0.0)[None] * lg[:, None, None]), 0.0)
    intra = jnp.einsum('bhncd,bhnmd->bhncm', qc, kc) * decay[None, :, None]
    o_intra = jnp.einsum('bhncm,bhnme->bhnce', intra, vc)
    k_decay = jnp.exp((C - 1.0 - idx)[None, :] * lg[:, None])
    kv = jnp.einsum('bhncd,bhnce->bhnde', kc * k_decay[None, :, None, :, None], vc)
    chunk_decay = jnp.exp(C * lg)[None, :, None, None]

    def step(state, kv_i):
        return chunk_decay * state + kv_i, state

    _, r_prev = lax.scan(step, jnp.zeros((B, RET_HEADS, HEAD_DIM, HEAD_DIM), jnp.float32),
                         jnp.moveaxis(kv, 2, 0))
    r_prev = jnp.moveaxis(r_prev, 0, 2)
    q_decay = jnp.exp((idx + 1.0)[None, :] * lg[:, None])
    o_cross = jnp.einsum('bhncd,bhnde->bhnce', qc * q_decay[None, :, None, :, None], r_prev)
    o = (o_intra + o_cross).reshape(B, RET_HEADS, T, HEAD_DIM)
    o = o * lax.rsqrt(jnp.mean(o * o, axis=-1, keepdims=True) + EPS)
    o = o.transpose(0, 2, 1, 3).reshape(B, T, RET_WIDTH)
    return (jax.nn.silu(g.astype(jnp.float32)) * o).astype(g.dtype)


def setup_inputs(seed: int = 0) -> dict:
    key = jax.random.key(seed)
    ks = jax.random.split(key, 32)
    f32 = jnp.float32
    D, F, L = D_MODEL, D_FF, DEPTH
    nrm = lambda k, shape, s: jax.random.normal(k, shape, f32) * s
    x = nrm(ks[0], (BATCH, SEQ, D), 1.0)
    c = nrm(ks[1], (BATCH, D), 1.0)
    positions = (jnp.arange(SEQ, dtype=jnp.int32)[None, :]
                 + jax.random.randint(ks[2], (BATCH, 1), 0, 1024, dtype=jnp.int32))
    u = jax.random.uniform(ks[20], (L, REC_WIDTH), f32, 0.9, 0.999)
    p = u ** (1.0 / REC_C)
    rec_lambda = jnp.log(p) - jnp.log1p(-p)
    return {
        "x": x,
        "c": c,
        "positions": positions,
        "norm_ffn1_g": 1.0 + nrm(ks[3], (L, D), 0.02),
        "norm_mix_g": 1.0 + nrm(ks[4], (L, D), 0.02),
        "norm_ffn2_g": 1.0 + nrm(ks[5], (L, D), 0.02),
        "ada_w": nrm(ks[6], (L, D, N_MOD * D), 0.5 * D ** -0.5),
        "ada_b": nrm(ks[7], (L, N_MOD * D), 0.02),
        "ffn1_w_in": nrm(ks[8], (L, D, 2 * F), D ** -0.5),
        "ffn1_w_out": nrm(ks[9], (L, F, D), F ** -0.5),
        "ffn2_w_in": nrm(ks[10], (L, D, 2 * F), D ** -0.5),
        "ffn2_w_out": nrm(ks[11], (L, F, D), F ** -0.5),
        "w_in": nrm(ks[12], (L, D, IN_WIDTH), D ** -0.5),
        "w_out": nrm(ks[13], (L, MIX_WIDTH, D), MIX_WIDTH ** -0.5),
        "rec_conv_w": nrm(ks[14], (L, REC_CONV, REC_WIDTH), REC_CONV ** -0.5),
        "rec_conv_b": nrm(ks[15], (L, REC_WIDTH), 0.02),
        "rec_gate_a_w": nrm(ks[16], (L, REC_HEADS, HEAD_DIM, HEAD_DIM), HEAD_DIM ** -0.5),
        "rec_gate_a_b": nrm(ks[17], (L, REC_WIDTH), 0.02),
        "rec_gate_x_w": nrm(ks[18], (L, REC_HEADS, HEAD_DIM, HEAD_DIM), HEAD_DIM ** -0.5),
        "rec_gate_x_b": nrm(ks[19], (L, REC_WIDTH), 0.02),
        "rec_lambda": rec_lambda,
        "diff_lambda_q1": nrm(ks[21], (L, DIFF_SUB), 0.1),
        "diff_lambda_k1": nrm(ks[22], (L, DIFF_SUB), 0.1),
        "diff_lambda_q2": nrm(ks[23], (L, DIFF_SUB), 0.1),
        "diff_lambda_k2": nrm(ks[24], (L, DIFF_SUB), 0.1),
        "diff_subln_g": 1.0 + nrm(ks[25], (L, HEAD_DIM), 0.02),
        "final_norm_g": 1.0 + nrm(ks[26], (D,), 0.02),
    }


def reference(x, c, positions, norm_ffn1_g, norm_mix_g, norm_ffn2_g, ada_w, ada_b,
              ffn1_w_in, ffn1_w_out, ffn2_w_in, ffn2_w_out, w_in, w_out,
              rec_conv_w, rec_conv_b, rec_gate_a_w, rec_gate_a_b, rec_gate_x_w, rec_gate_x_b,
              rec_lambda, diff_lambda_q1, diff_lambda_k1, diff_lambda_q2, diff_lambda_k2,
              diff_subln_g, final_norm_g):
    split_at = [int(s) for s in np.cumsum(IN_SIZES)[:-1]]
    silu_c = jax.nn.silu(c)
    for l in range(DEPTH):
        mod = (silu_c @ ada_w[l] + ada_b[l])[:, None, :]
        sh1, sc1, g1, sh2, sc2, g2, sh3, sc3, g3 = jnp.split(mod, N_MOD, axis=-1)
        h = modulate(rmsnorm(x, norm_ffn1_g[l]), sh1, sc1)
        x = x + 0.5 * g1 * swiglu(h, ffn1_w_in[l], ffn1_w_out[l])
        h = modulate(rmsnorm(x, norm_mix_g[l]), sh2, sc2)
        rx, rg, dq, dk, dv, tq, tk, tv, tg = jnp.split(h @ w_in[l], split_at, axis=-1)
        y_rec = rglru_group(rx, rg, rec_conv_w[l], rec_conv_b[l], rec_gate_a_w[l], rec_gate_a_b[l],
                            rec_gate_x_w[l], rec_gate_x_b[l], rec_lambda[l])
        lambda_init = 0.8 - 0.6 * math.exp(-0.3 * l)
        y_diff = diff_attention_group(dq, dk, dv, positions, diff_lambda_q1[l], diff_lambda_k1[l],
                                      diff_lambda_q2[l], diff_lambda_k2[l], diff_subln_g[l], lambda_init)
        y_ret = retention_group(tq, tk, tv, tg, positions)
        y = jnp.concatenate([y_rec, y_diff, y_ret], axis=-1) @ w_out[l]
        x = x + g2 * y
        h = modulate(rmsnorm(x, norm_ffn2_g[l]), sh3, sc3)
        x = x + 0.5 * g3 * swiglu(h, ffn2_w_in[l], ffn2_w_out[l])
    return rmsnorm(x, final_norm_g)
```

```python
import functools
import math

import jax
import jax.numpy as jnp
from jax import lax
from jax.experimental import pallas as pl
from jax.experimental.pallas import tpu as pltpu

F32 = jnp.float32
BF16 = jnp.bfloat16

D_MODEL = 1024
BATCH = 4
SEQ = 4096
DEPTH = 4
N_TOK = BATCH * SEQ
HEAD_DIM = 64
REC_WIDTH = 384
REC_HEADS = 6
REC_C = 8.0
DIFF_WIDTH = 384
DIFF_HEADS = 6
DIFF_SUB = 32
ROPE_THETA = 500000.0
ROPE_DIM = 8
RET_WIDTH = 256
RET_HEADS = 4
RET_ROT_BASE = 10000.0
IN_WIDTH = 2944
D_FF = 2816
N_MOD = 9
EPS = 1e-6

LANES = 128
MXU_DIM = 256
VMEM_LIMIT_BYTES = 56 * 1024 * 1024

TM = 512
FF_CHUNK = MXU_DIM
N_FF_CHUNKS = D_FF // FF_CHUNK
ATT_BLK = 256
RET_BLK = 256
LRU_BLK = 256
TAB_BLK = 1024
NEG_BIG = -1e30


def _sigmoid(x):
    return 1.0 / (1.0 + jnp.exp(-x))


def _norm_mod(x, g, shift, scale):
    ms = jnp.mean(x * x, axis=-1, keepdims=True)
    y = x * lax.rsqrt(ms + EPS) * g
    return y * (1.0 + scale) + shift


def _resident(shape):
    nd = len(shape)
    return pl.BlockSpec(shape, lambda *_: (0,) * nd, pipeline_mode=pl.Buffered(1))


def _params(n_axes, vmem=VMEM_LIMIT_BYTES):
    return pltpu.CompilerParams(dimension_semantics=("arbitrary",) * n_axes, vmem_limit_bytes=vmem)


def _ada_kernel(c_ref, w_ref, b_ref, o_ref):
    c = c_ref[...]
    s = c * _sigmoid(c)
    o_ref[0] = jnp.dot(s, w_ref[0], preferred_element_type=F32, precision=lax.Precision.HIGHEST) + b_ref[0]


def _ada_call(c, ada_w, ada_b):
    tn = 1536
    nmod = N_MOD * D_MODEL
    return pl.pallas_call(
        _ada_kernel,
        grid=(DEPTH, nmod // tn),
        in_specs=[
            pl.BlockSpec((BATCH, D_MODEL), lambda l, j: (0, 0)),
            pl.BlockSpec((1, D_MODEL, tn), lambda l, j: (l, 0, j)),
            pl.BlockSpec((1, 1, tn), lambda l, j: (l, 0, j)),
        ],
        out_specs=pl.BlockSpec((1, BATCH, tn), lambda l, j: (l, 0, j)),
        out_shape=jax.ShapeDtypeStruct((DEPTH, BATCH, nmod), F32),
        compiler_params=_params(2),
        name="ada_mod",
    )(c, ada_w, ada_b.reshape(DEPTH, 1, nmod))


def _tables_kernel(pos_ref, invd_ref, sgnd_ref, invr_ref, sgnr_ref, cd_ref, sd_ref, cr_ref, sr_ref):
    pos = pos_ref[...].astype(F32)
    ang_d = pos * invd_ref[...]
    cd_ref[...] = jnp.cos(ang_d)
    sd_ref[...] = jnp.sin(ang_d) * sgnd_ref[...]
    ang_r = pos * invr_ref[...]
    cr_ref[...] = jnp.cos(ang_r)
    sr_ref[...] = jnp.sin(ang_r) * sgnr_ref[...]


def _rotary_consts():
    lane = jnp.arange(LANES)
    half_d = ROPE_DIM // 2
    inv_d = jnp.power(jnp.float32(ROPE_THETA), -jnp.arange(half_d, dtype=F32) * (2.0 / ROPE_DIM))
    cd = lane % DIFF_SUB
    inv_d_lane = jnp.where(cd < ROPE_DIM, inv_d[cd % half_d], 0.0)
    sgn_d_lane = jnp.where(cd < half_d, -1.0, jnp.where(cd < ROPE_DIM, 1.0, 0.0))
    half_r = HEAD_DIM // 2
    inv_r = jnp.power(jnp.float32(RET_ROT_BASE), -jnp.arange(half_r, dtype=F32) * (2.0 / HEAD_DIM))
    cr = lane % HEAD_DIM
    inv_r_lane = inv_r[cr % half_r]
    sgn_r_lane = jnp.where(cr < half_r, -1.0, 1.0)
    row = lambda v: v.astype(F32).reshape(1, LANES)
    return row(inv_d_lane), row(sgn_d_lane), row(inv_r_lane), row(sgn_r_lane)


def _tables_call(positions):
    consts = _rotary_consts()
    vec = pl.BlockSpec((1, LANES), lambda i: (0, 0))
    tab = pl.BlockSpec((TAB_BLK, LANES), lambda i: (i, 0))
    return pl.pallas_call(
        _tables_kernel,
        grid=(N_TOK // TAB_BLK,),
        in_specs=[pl.BlockSpec((TAB_BLK, 1), lambda i: (i, 0)), vec, vec, vec, vec],
        out_specs=[tab, tab, tab, tab],
        out_shape=[jax.ShapeDtypeStruct((N_TOK, LANES), F32)] * 4,
        compiler_params=_params(1),
        name="rotary_tables",
    )(positions.reshape(N_TOK, 1), *consts)


def _ffn_kernel(x_ref, mod_ref, g_ref, wg_ref, wu_ref, wo_ref, o_ref, acc_ref, *, mod_row):
    x = x_ref[...]
    shift = mod_ref[0, mod_row:mod_row + 1, :]
    scale = mod_ref[0, mod_row + 1:mod_row + 2, :]
    gate = mod_ref[0, mod_row + 2:mod_row + 3, :]
    h = _norm_mod(x, g_ref[...], shift, scale).astype(BF16)
    for c in range(N_FF_CHUNKS):
        hg = jnp.dot(h, wg_ref[c], preferred_element_type=F32)
        hu = jnp.dot(h, wu_ref[c], preferred_element_type=F32)
        a = (hg * _sigmoid(hg) * hu).astype(BF16)
        part = jnp.dot(a, wo_ref[c], preferred_element_type=F32)
        if c == 0:
            acc_ref[...] = part
        else:
            acc_ref[...] += part
    o_ref[...] = x + (0.5 * gate) * acc_ref[...]


def _ffn_call(x, mod_l, g, wg, wu, wo, mod_row):
    tiles_per_batch = SEQ // TM
    tok = pl.BlockSpec((TM, D_MODEL), lambda i: (i, 0))
    return pl.pallas_call(
        functools.partial(_ffn_kernel, mod_row=mod_row),
        grid=(N_TOK // TM,),
        in_specs=[
            tok,
            pl.BlockSpec((1, N_MOD, D_MODEL), lambda i: (i // tiles_per_batch, 0, 0)),
            pl.BlockSpec((1, D_MODEL), lambda i: (0, 0)),
            _resident(wg.shape), _resident(wu.shape), _resident(wo.shape),
        ],
        out_specs=tok,
        out_shape=jax.ShapeDtypeStruct((N_TOK, D_MODEL), F32),
        scratch_shapes=[pltpu.VMEM((TM, D_MODEL), F32)],
        compiler_params=_params(1),
        name="ffn_swiglu",
    )(x, mod_l, g.reshape(1, D_MODEL), wg, wu, wo)


def _ffn_weights(w_in, w_out):
    wg = w_in[:, :D_FF].reshape(D_MODEL, N_FF_CHUNKS, FF_CHUNK).transpose(1, 0, 2).astype(BF16)
    wu = w_in[:, D_FF:].reshape(D_MODEL, N_FF_CHUNKS, FF_CHUNK).transpose(1, 0, 2).astype(BF16)
    wo = w_out.reshape(N_FF_CHUNKS, FF_CHUNK, D_MODEL).astype(BF16)
    return wg, wu, wo


def _rope(x, cos, sin, lo_mask, shift_up, shift_dn):
    outs = []
    for j in range(x.shape[1] // LANES):
        xj = x[:, j * LANES:(j + 1) * LANES]
        partner = jnp.where(lo_mask, pltpu.roll(xj, shift_up, 1), pltpu.roll(xj, shift_dn, 1))
        outs.append(xj * cos + partner * sin)
    return jnp.concatenate(outs, axis=1)


def _proj_kernel(x_ref, mod_ref, g_ref, w_ref, cd_ref, sd_ref, cr_ref, sr_ref,
                 rx_ref, rg_ref, qt_ref, k_ref, vt_ref, tq_ref, tk_ref, tv_ref, tg_ref):
    x = x_ref[...]
    shift = mod_ref[0, 3:4, :]
    scale = mod_ref[0, 4:5, :]
    h = _norm_mod(x, g_ref[...], shift, scale).astype(BF16)

    def proj(lo, width):
        return jnp.dot(h, w_ref[:, lo:lo + width], preferred_element_type=F32)

    lane = lax.broadcasted_iota(jnp.int32, (TM, LANES), 1)
    lo_d = (lane & (DIFF_SUB - 1)) < (ROPE_DIM // 2)
    lo_r = (lane & (HEAD_DIM - 1)) < (HEAD_DIM // 2)
    cd, sd, cr, sr = cd_ref[...], sd_ref[...], cr_ref[...], sr_ref[...]
    rope_d = lambda v: _rope(v, cd, sd, lo_d, LANES - ROPE_DIM // 2, ROPE_DIM // 2)
    rope_r = lambda v: _rope(v, cr, sr, lo_r, LANES - HEAD_DIM // 2, HEAD_DIM // 2)

    rx_ref[...] = proj(0, REC_WIDTH)
    rg_ref[...] = proj(384, REC_WIDTH)
    q = rope_d(proj(768, DIFF_WIDTH)) * (DIFF_SUB ** -0.5)
    qt = q.T
    k_ref[...] = rope_d(proj(1152, DIFF_WIDTH)).astype(BF16)
    vt = proj(1536, DIFF_WIDTH).T
    for j in range(TM // ATT_BLK):
        qt_ref[0, j] = qt[:, j * ATT_BLK:(j + 1) * ATT_BLK].astype(BF16)
        vt_ref[0, j] = vt[:, j * ATT_BLK:(j + 1) * ATT_BLK].astype(BF16)
    tq_ref[...] = rope_r(proj(1920, RET_WIDTH)).astype(BF16)
    tk_ref[...] = (rope_r(proj(2176, RET_WIDTH)) * (HEAD_DIM ** -0.5)).astype(BF16)
    tv_ref[...] = proj(2432, RET_WIDTH).astype(BF16)
    tg_ref[...] = proj(2688, RET_WIDTH)


def _proj_call(x, mod_l, g, w, tables):
    tiles_per_batch = SEQ // TM
    blocks_per_tile = TM // ATT_BLK
    n_blk = SEQ // ATT_BLK
    tok = lambda w_, : pl.BlockSpec((TM, w_), lambda i: (i, 0))
    tspec = pl.BlockSpec((1, blocks_per_tile, DIFF_WIDTH, ATT_BLK),
                         lambda i: (i // tiles_per_batch, i % tiles_per_batch, 0, 0))
    tshape = jax.ShapeDtypeStruct((BATCH, n_blk, DIFF_WIDTH, ATT_BLK), BF16)
    tok_shape = lambda w_, dt: jax.ShapeDtypeStruct((N_TOK, w_), dt)
    return pl.pallas_call(
        _proj_kernel,
        grid=(N_TOK // TM,),
        in_specs=[
            tok(D_MODEL),
            pl.BlockSpec((1, N_MOD, D_MODEL), lambda i: (i // tiles_per_batch, 0, 0)),
            pl.BlockSpec((1, D_MODEL), lambda i: (0, 0)),
            _resident(w.shape),
            tok(LANES), tok(LANES), tok(LANES), tok(LANES),
        ],
        out_specs=[tok(REC_WIDTH), tok(REC_WIDTH), tspec, tok(DIFF_WIDTH), tspec,
                   tok(RET_WIDTH), tok(RET_WIDTH), tok(RET_WIDTH), tok(RET_WIDTH)],
        out_shape=[tok_shape(REC_WIDTH, F32), tok_shape(REC_WIDTH, F32), tshape, tok_shape(DIFF_WIDTH, BF16), tshape,
                   tok_shape(RET_WIDTH, BF16), tok_shape(RET_WIDTH, BF16), tok_shape(RET_WIDTH, BF16),
                   tok_shape(RET_WIDTH, F32)],
        compiler_params=_params(1),
        name="mixer_in_proj",
    )(x, mod_l, g.reshape(1, D_MODEL), w, *tables)


def _rglru_kernel(rx_ref, rg_ref, cw_ref, cb_ref, wa_ref, ba_ref, wx_ref, bx_ref, lam_ref,
                  y_ref, xe_ref, h_ref):
    t = pl.program_id(1)

    @pl.when(t == 0)
    def _():
        xe_ref[0:8, :] = jnp.zeros((8, REC_WIDTH), F32)
        h_ref[...] = jnp.zeros_like(h_ref)

    x = rx_ref[...]
    xe_ref[8:8 + LRU_BLK, :] = x
    cw = cw_ref[...]
    xc = (cb_ref[...] + cw[3:4, :] * x
          + cw[2:3, :] * xe_ref[7:7 + LRU_BLK, :]
          + cw[1:2, :] * xe_ref[6:6 + LRU_BLK, :]
          + cw[0:1, :] * xe_ref[5:5 + LRU_BLK, :])
    xe_ref[0:8, :] = x[LRU_BLK - 8:, :]

    xb = xc.astype(BF16)
    r = _sigmoid(jnp.dot(xb, wa_ref[...], preferred_element_type=F32) + ba_ref[...])
    gi = _sigmoid(jnp.dot(xb, wx_ref[...], preferred_element_type=F32) + bx_ref[...])
    nl = -lam_ref[...]
    softplus = jnp.maximum(nl, 0.0) + jnp.log1p(jnp.exp(-jnp.abs(nl)))
    log_a = -REC_C * r * softplus
    a = jnp.exp(log_a)
    b = jnp.sqrt(-jnp.tanh(log_a) * (a * a + 1.0)) * (gi * xc)

    row = lax.broadcasted_iota(jnp.int32, (LRU_BLK, REC_WIDTH), 0)
    s = 1
    while s < LRU_BLK:
        valid = row >= s
        a_prev = pltpu.roll(a, s, 0)
        b_prev = pltpu.roll(b, s, 0)
        b = jnp.where(valid, a * b_prev, 0.0) + b
        a = jnp.where(valid, a * a_prev, a)
        s *= 2
    h = a * h_ref[...] + b
    h_ref[...] = h[LRU_BLK - 1:, :]

    gate = rg_ref[...]
    cdf = 0.5 * (1.0 + jnp.tanh(math.sqrt(2.0 / math.pi) * (gate + 0.044715 * (gate * gate * gate))))
    y_ref[...] = (gate * cdf * h).astype(BF16)


def _rglru_call(rx, rg, conv_w, conv_b, wa_bd, ba, wx_bd, bx, lam):
    nt = SEQ // LRU_BLK
    tok = pl.BlockSpec((LRU_BLK, REC_WIDTH), lambda b, t: (b * nt + t, 0))
    vec = pl.BlockSpec((1, REC_WIDTH), lambda b, t: (0, 0))
    mat = pl.BlockSpec((REC_WIDTH, REC_WIDTH), lambda b, t: (0, 0))
    row = lambda v: v.reshape(1, REC_WIDTH)
    return pl.pallas_call(
        _rglru_kernel,
        grid=(BATCH, nt),
        in_specs=[tok, tok, pl.BlockSpec((4, REC_WIDTH), lambda b, t: (0, 0)), vec, mat, vec, mat, vec, vec],
        out_specs=tok,
        out_shape=jax.ShapeDtypeStruct((N_TOK, REC_WIDTH), BF16),
        scratch_shapes=[pltpu.VMEM((LRU_BLK + 8, REC_WIDTH), F32), pltpu.VMEM((1, REC_WIDTH), F32)],
        compiler_params=_params(2),
        name="rglru",
    )(rx, rg, conv_w, row(conv_b), wa_bd, row(ba), wx_bd, row(bx), row(lam))


def _block_diag(w):
    h, d, _ = w.shape
    eye = jnp.eye(h, dtype=w.dtype)
    return jnp.einsum('hij,hk->hikj', w, eye).reshape(h * d, h * d)


def _diff_kernel(lam_ref, qt_ref, k_ref, vt_ref, g_ref, o_ref, qm_ref, m_ref, l_ref, acc_ref, *, lambda_init):
    i = pl.program_id(2)
    qt = qt_ref[0, 0]
    rowq = lax.broadcasted_iota(jnp.int32, (LANES, ATT_BLK), 0)
    for idx in range(4):
        band = (rowq >= idx * DIFF_SUB) & (rowq < (idx + 1) * DIFF_SUB)
        qm_ref[idx] = jnp.where(band, qt, jnp.zeros_like(qt))
        m_ref[idx] = jnp.full((1, ATT_BLK), NEG_BIG, F32)
        l_ref[idx] = jnp.zeros((1, ATT_BLK), F32)
        acc_ref[idx] = jnp.zeros((HEAD_DIM, ATT_BLK), F32)

    def step(j, masked):
        kb = k_ref[0, pl.ds(pl.multiple_of(j * ATT_BLK, ATT_BLK), ATT_BLK), :]
        vb = vt_ref[0, j]
        if masked:
            kk = lax.broadcasted_iota(jnp.int32, (ATT_BLK, ATT_BLK), 0)
            qq = lax.broadcasted_iota(jnp.int32, (ATT_BLK, ATT_BLK), 1)
            causal = kk <= qq
        for idx in range(4):
            st = jnp.dot(kb, qm_ref[idx], preferred_element_type=F32)
            if masked:
                st = jnp.where(causal, st, NEG_BIG)
            m_old = m_ref[idx]
            m_new = jnp.maximum(m_old, jnp.max(st, axis=0, keepdims=True))
            alpha = jnp.exp(m_old - m_new)
            p = jnp.exp(st - m_new)
            l_ref[idx] = alpha * l_ref[idx] + jnp.sum(p, axis=0, keepdims=True)
            vh = vb[(idx // 2) * HEAD_DIM:(idx // 2 + 1) * HEAD_DIM, :]
            acc_ref[idx] = alpha * acc_ref[idx] + jnp.dot(vh, p.astype(BF16), preferred_element_type=F32)
            m_ref[idx] = m_new

    def body(j, carry):
        step(j, False)
        return carry

    lax.fori_loop(0, i, body, 0)
    step(i, True)

    lv = lam_ref[...]
    lam = (jnp.exp(jnp.sum(lv[0:1] * lv[1:2], axis=-1, keepdims=True))
           - jnp.exp(jnp.sum(lv[2:3] * lv[3:4], axis=-1, keepdims=True)) + lambda_init)
    outs = []
    for hh in range(2):
        o = acc_ref[2 * hh] / l_ref[2 * hh] - lam * (acc_ref[2 * hh + 1] / l_ref[2 * hh + 1])
        ms = jnp.mean(o * o, axis=0, keepdims=True)
        outs.append(o * lax.rsqrt(ms + EPS) * g_ref[...] * (1.0 - lambda_init))
    o_ref[...] = jnp.concatenate(outs, axis=0).T.astype(BF16)


def _diff_call(lam_vecs, qt, k, vt, subln_g, lambda_init):
    nq = SEQ // ATT_BLK
    groups = DIFF_WIDTH // LANES
    return pl.pallas_call(
        functools.partial(_diff_kernel, lambda_init=lambda_init),
        grid=(BATCH, groups, nq),
        in_specs=[
            pl.BlockSpec((4, LANES), lambda b, g, i: (0, 0)),
            pl.BlockSpec((1, 1, LANES, ATT_BLK), lambda b, g, i: (b, i, g, 0)),
            pl.BlockSpec((1, SEQ, LANES), lambda b, g, i: (b, 0, g)),
            pl.BlockSpec((1, nq, LANES, ATT_BLK), lambda b, g, i: (b, 0, g, 0)),
            pl.BlockSpec((HEAD_DIM, 1), lambda b, g, i: (0, 0)),
        ],
        out_specs=pl.BlockSpec((ATT_BLK, LANES), lambda b, g, i: (b * nq + i, g)),
        out_shape=jax.ShapeDtypeStruct((N_TOK, DIFF_WIDTH), BF16),
        scratch_shapes=[
            pltpu.VMEM((4, LANES, ATT_BLK), BF16),
            pltpu.VMEM((4, 1, ATT_BLK), F32),
            pltpu.VMEM((4, 1, ATT_BLK), F32),
            pltpu.VMEM((4, HEAD_DIM, ATT_BLK), F32),
        ],
        compiler_params=_params(3),
        name="diff_attn",
    )(lam_vecs, qt, k.reshape(BATCH, SEQ, DIFF_WIDTH), vt, subln_g.reshape(HEAD_DIM, 1))


def _ret_kernel(q_ref, k_ref, v_ref, g_ref, dec_ref, kdec_ref, qdec_ref, cdec_ref, bmask_ref, o_ref, s_ref):
    c = pl.program_id(2)

    @pl.when(c == 0)
    def _():
        s_ref[...] = jnp.zeros_like(s_ref)

    q = q_ref[...]
    k = k_ref[...]
    v = v_ref[...]
    lane = lax.broadcasted_iota(jnp.int32, (RET_BLK, LANES), 1)
    first = lane < HEAD_DIM
    state = s_ref[...]
    o = jnp.dot(q, state.astype(BF16), preferred_element_type=F32) * qdec_ref[0]
    for hh in range(2):
        sel = first if hh == 0 else jnp.logical_not(first)
        qm = jnp.where(sel, q, jnp.zeros_like(q))
        sc = lax.dot_general(qm, k, (((1,), (1,)), ((), ())), preferred_element_type=F32) * dec_ref[0, hh]
        oi = jnp.dot(sc.astype(BF16), v, preferred_element_type=F32)
        o = o + jnp.where(sel, oi, 0.0)
    kd_t = (k.astype(F32) * kdec_ref[0]).T.astype(BF16)
    kv = jnp.dot(kd_t, v, preferred_element_type=F32)
    s_ref[...] = state * cdec_ref[0] + kv * bmask_ref[...]

    o2 = o * o
    ms0 = jnp.sum(jnp.where(first, o2, 0.0), axis=-1, keepdims=True) * (1.0 / HEAD_DIM)
    ms1 = jnp.sum(jnp.where(first, 0.0, o2), axis=-1, keepdims=True) * (1.0 / HEAD_DIM)
    rs = jnp.where(first, lax.rsqrt(ms0 + EPS), lax.rsqrt(ms1 + EPS))
    gate = g_ref[...]
    o_ref[...] = (gate * _sigmoid(gate) * (o * rs)).astype(BF16)


def _ret_consts():
    c = RET_BLK
    lg = jnp.log1p(-jnp.exp2(-5.0 - jnp.arange(RET_HEADS, dtype=F32)))
    idx = jnp.arange(c, dtype=F32)
    rel = idx[:, None] - idx[None, :]
    decay = jnp.where(rel >= 0, jnp.exp(jnp.maximum(rel, 0.0)[None] * lg[:, None, None]), 0.0)
    k_decay = jnp.exp((c - 1.0 - idx)[None, :] * lg[:, None])
    q_decay = jnp.exp((idx + 1.0)[None, :] * lg[:, None])
    chunk_decay = jnp.exp(c * lg)
    pairs = RET_HEADS // 2
    lanes = lambda t: jnp.repeat(t.reshape(pairs, 2, c).transpose(0, 2, 1), HEAD_DIM, axis=2)
    cdec = jnp.broadcast_to(jnp.repeat(chunk_decay.reshape(pairs, 2), HEAD_DIM, axis=1)[:, :, None],
                            (pairs, LANES, LANES))
    blk = jnp.arange(LANES) // HEAD_DIM
    bmask = (blk[:, None] == blk[None, :]).astype(F32)
    return decay.reshape(pairs, 2, c, c), lanes(k_decay), lanes(q_decay), cdec, bmask


def _ret_call(tq, tk, tv, tg, consts):
    decay, kdec, qdec, cdec, bmask = consts
    nc = SEQ // RET_BLK
    tok = pl.BlockSpec((RET_BLK, LANES), lambda b, p, c: (b * nc + c, p))
    return pl.pallas_call(
        _ret_kernel,
        grid=(BATCH, RET_HEADS // 2, nc),
        in_specs=[
            tok, tok, tok, tok,
            pl.BlockSpec((1, 2, RET_BLK, RET_BLK), lambda b, p, c: (p, 0, 0, 0)),
            pl.BlockSpec((1, RET_BLK, LANES), lambda b, p, c: (p, 0, 0)),
            pl.BlockSpec((1, RET_BLK, LANES), lambda b, p, c: (p, 0, 0)),
            pl.BlockSpec((1, LANES, LANES), lambda b, p, c: (p, 0, 0)),
            pl.BlockSpec((LANES, LANES), lambda b, p, c: (0, 0)),
        ],
        out_specs=tok,
        out_shape=jax.ShapeDtypeStruct((N_TOK, RET_WIDTH), BF16),
        scratch_shapes=[pltpu.VMEM((LANES, LANES), F32)],
        compiler_params=_params(3),
        name="retention",
    )(tq, tk, tv, tg, decay, kdec, qdec, cdec, bmask)


def _outproj_kernel(x_ref, mod_ref, yr_ref, yd_ref, yt_ref, w_ref, o_ref):
    gate = mod_ref[0, 5:6, :]
    y = jnp.dot(yr_ref[...], w_ref[0:REC_WIDTH, :], preferred_element_type=F32)
    y = y + jnp.dot(yd_ref[...], w_ref[REC_WIDTH:REC_WIDTH + DIFF_WIDTH, :], preferred_element_type=F32)
    y = y + jnp.dot(yt_ref[...], w_ref[REC_WIDTH + DIFF_WIDTH:, :], preferred_element_type=F32)
    o_ref[...] = x_ref[...] + gate * y


def _outproj_call(x, mod_l, y_rec, y_diff, y_ret, w):
    tiles_per_batch = SEQ // TM
    tok = lambda w_: pl.BlockSpec((TM, w_), lambda i: (i, 0))
    return pl.pallas_call(
        _outproj_kernel,
        grid=(N_TOK // TM,),
        in_specs=[
            tok(D_MODEL),
            pl.BlockSpec((1, N_MOD, D_MODEL), lambda i: (i // tiles_per_batch, 0, 0)),
            tok(REC_WIDTH), tok(DIFF_WIDTH), tok(RET_WIDTH),
            _resident(w.shape),
        ],
        out_specs=tok(D_MODEL),
        out_shape=jax.ShapeDtypeStruct((N_TOK, D_MODEL), F32),
        compiler_params=_params(1),
        name="mixer_out_proj",
    )(x, mod_l, y_rec, y_diff, y_ret, w)


def _final_norm_kernel(x_ref, g_ref, o_ref):
    x = x_ref[...]
    ms = jnp.mean(x * x, axis=-1, keepdims=True)
    o_ref[...] = x * lax.rsqrt(ms + EPS) * g_ref[...]


def _final_norm_call(x, g):
    tok = pl.BlockSpec((TAB_BLK, D_MODEL), lambda i: (i, 0))
    return pl.pallas_call(
        _final_norm_kernel,
        grid=(N_TOK // TAB_BLK,),
        in_specs=[tok, pl.BlockSpec((1, D_MODEL), lambda i: (0, 0))],
        out_specs=tok,
        out_shape=jax.ShapeDtypeStruct((N_TOK, D_MODEL), F32),
        compiler_params=_params(1),
        name="final_norm",
    )(x, g.reshape(1, D_MODEL))


def kernel(x, c, positions, norm_ffn1_g, norm_mix_g, norm_ffn2_g, ada_w, ada_b, ffn1_w_in, ffn1_w_out, ffn2_w_in, ffn2_w_out, w_in, w_out, rec_conv_w, rec_conv_b, rec_gate_a_w, rec_gate_a_b, rec_gate_x_w, rec_gate_x_b, rec_lambda, diff_lambda_q1, diff_lambda_k1, diff_lambda_q2, diff_lambda_k2, diff_subln_g, final_norm_g):
    xf = x.reshape(N_TOK, D_MODEL)
    mod = _ada_call(c, ada_w, ada_b).reshape(DEPTH, BATCH, N_MOD, D_MODEL)
    tables = _tables_call(positions)
    ret_consts = _ret_consts()
    for l in range(DEPTH):
        xf = _ffn_call(xf, mod[l], norm_ffn1_g[l], *_ffn_weights(ffn1_w_in[l], ffn1_w_out[l]), mod_row=0)

        rx, rg, qt, k, vt, tq, tk, tv, tg = _proj_call(xf, mod[l], norm_mix_g[l], w_in[l].astype(BF16), tables)
        y_rec = _rglru_call(rx, rg, rec_conv_w[l], rec_conv_b[l],
                            _block_diag(rec_gate_a_w[l]).astype(BF16), rec_gate_a_b[l],
                            _block_diag(rec_gate_x_w[l]).astype(BF16), rec_gate_x_b[l], rec_lambda[l])
        lambda_init = 0.8 - 0.6 * math.exp(-0.3 * l)
        lam_vecs = jnp.stack([diff_lambda_q1[l], diff_lambda_k1[l], diff_lambda_q2[l], diff_lambda_k2[l]])
        lam_vecs = jnp.pad(lam_vecs, ((0, 0), (0, LANES - DIFF_SUB)))
        y_diff = _diff_call(lam_vecs, qt, k, vt, diff_subln_g[l], lambda_init)
        y_ret = _ret_call(tq, tk, tv, tg, ret_consts)
        xf = _outproj_call(xf, mod[l], y_rec, y_diff, y_ret, w_out[l].astype(BF16))

        xf = _ffn_call(xf, mod[l], norm_ffn2_g[l], *_ffn_weights(ffn2_w_in[l], ffn2_w_out[l]), mod_row=6)
    return _final_norm_call(xf, final_norm_g).reshape(BATCH, SEQ, D_MODEL)
```

```python
import functools
import math

import jax
import jax.numpy as jnp
from jax import lax
from jax.experimental import pallas as pl
from jax.experimental.pallas import tpu as pltpu

F32 = jnp.float32
BF16 = jnp.bfloat16

D_MODEL = 1024
BATCH = 4
SEQ = 4096
DEPTH = 4
N_TOK = BATCH * SEQ
HEAD_DIM = 64
REC_WIDTH = 384
REC_HEADS = 6
REC_C = 8.0
DIFF_WIDTH = 384
DIFF_HEADS = 6
DIFF_SUB = 32
ROPE_THETA = 500000.0
ROPE_DIM = 8
RET_WIDTH = 256
RET_HEADS = 4
RET_ROT_BASE = 10000.0
IN_WIDTH = 2944
D_FF = 2816
N_MOD = 9
EPS = 1e-6

LANES = 128
MXU_DIM = 256
VMEM_LIMIT_BYTES = 56 * 1024 * 1024

TM = 512
FF_CHUNK = MXU_DIM
N_FF_CHUNKS = D_FF // FF_CHUNK
ATT_BLK = 256
RET_BLK = 256
LRU_BLK = 256
TAB_BLK = 1024
NEG_BIG = -1e30


def _sigmoid(x):
    return 1.0 / (1.0 + jnp.exp(-x))


def _norm_mod(x, g, shift, scale):
    ms = jnp.mean(x * x, axis=-1, keepdims=True)
    y = x * lax.rsqrt(ms + EPS) * g
    return y * (1.0 + scale) + shift


def _resident(shape):
    nd = len(shape)
    return pl.BlockSpec(shape, lambda *_: (0,) * nd, pipeline_mode=pl.Buffered(1))


def _params(n_axes, vmem=VMEM_LIMIT_BYTES):
    return pltpu.CompilerParams(dimension_semantics=("arbitrary",) * n_axes, vmem_limit_bytes=vmem)


def _ada_kernel(c_ref, w_ref, b_ref, o_ref):
    c = c_ref[...]
    s = c * _sigmoid(c)
    o_ref[0] = jnp.dot(s, w_ref[0], preferred_element_type=F32, precision=lax.Precision.HIGHEST) + b_ref[0]


def _ada_call(c, ada_w, ada_b):
    tn = 1536
    nmod = N_MOD * D_MODEL
    return pl.pallas_call(
        _ada_kernel,
        grid=(DEPTH, nmod // tn),
        in_specs=[
            pl.BlockSpec((BATCH, D_MODEL), lambda l, j: (0, 0)),
            pl.BlockSpec((1, D_MODEL, tn), lambda l, j: (l, 0, j)),
            pl.BlockSpec((1, 1, tn), lambda l, j: (l, 0, j)),
        ],
        out_specs=pl.BlockSpec((1, BATCH, tn), lambda l, j: (l, 0, j)),
        out_shape=jax.ShapeDtypeStruct((DEPTH, BATCH, nmod), F32),
        compiler_params=_params(2),
        name="ada_mod",
    )(c, ada_w, ada_b.reshape(DEPTH, 1, nmod))


def _tables_kernel(pos_ref, invd_ref, sgnd_ref, invr_ref, sgnr_ref, cd_ref, sd_ref, cr_ref, sr_ref):
    pos = pos_ref[...].astype(F32)
    ang_d = pos * invd_ref[...]
    cd_ref[...] = jnp.cos(ang_d)
    sd_ref[...] = jnp.sin(ang_d) * sgnd_ref[...]
    ang_r = pos * invr_ref[...]
    cr_ref[...] = jnp.cos(ang_r)
    sr_ref[...] = jnp.sin(ang_r) * sgnr_ref[...]


def _rotary_consts():
    lane = jnp.arange(LANES)
    half_d = ROPE_DIM // 2
    inv_d = jnp.power(jnp.float32(ROPE_THETA), -jnp.arange(half_d, dtype=F32) * (2.0 / ROPE_DIM))
    cd = lane % DIFF_SUB
    inv_d_lane = jnp.where(cd < ROPE_DIM, inv_d[cd % half_d], 0.0)
    sgn_d_lane = jnp.where(cd < half_d, -1.0, jnp.where(cd < ROPE_DIM, 1.0, 0.0))
    half_r = HEAD_DIM // 2
    inv_r = jnp.power(jnp.float32(RET_ROT_BASE), -jnp.arange(half_r, dtype=F32) * (2.0 / HEAD_DIM))
    cr = lane % HEAD_DIM
    inv_r_lane = inv_r[cr % half_r]
    sgn_r_lane = jnp.where(cr < half_r, -1.0, 1.0)
    row = lambda v: v.astype(F32).reshape(1, LANES)
    return row(inv_d_lane), row(sgn_d_lane), row(inv_r_lane), row(sgn_r_lane)


def _tables_call(positions):
    consts = _rotary_consts()
    vec = pl.BlockSpec((1, LANES), lambda i: (0, 0))
    tab = pl.BlockSpec((TAB_BLK, LANES), lambda i: (i, 0))
    return pl.pallas_call(
        _tables_kernel,
        grid=(N_TOK // TAB_BLK,),
        in_specs=[pl.BlockSpec((TAB_BLK, 1), lambda i: (i, 0)), vec, vec, vec, vec],
        out_specs=[tab, tab, tab, tab],
        out_shape=[jax.ShapeDtypeStruct((N_TOK, LANES), F32)] * 4,
        compiler_params=_params(1),
        name="rotary_tables",
    )(positions.reshape(N_TOK, 1), *consts)


def _ffn_kernel(x_ref, mod_ref, g_ref, wg_ref, wu_ref, wo_ref, o_ref, acc_ref, *, mod_row):
    x = x_ref[...]
    shift = mod_ref[0, mod_row:mod_row + 1, :]
    scale = mod_ref[0, mod_row + 1:mod_row + 2, :]
    gate = mod_ref[0, mod_row + 2:mod_row + 3, :]
    h = _norm_mod(x, g_ref[...], shift, scale).astype(BF16)
    for c in range(N_FF_CHUNKS):
        hg = jnp.dot(h, wg_ref[c], preferred_element_type=F32)
        hu = jnp.dot(h, wu_ref[c], preferred_element_type=F32)
        a = (hg * _sigmoid(hg) * hu).astype(BF16)
        part = jnp.dot(a, wo_ref[c], preferred_element_type=F32)
        if c == 0:
            acc_ref[...] = part
        else:
            acc_ref[...] += part
    o_ref[...] = x + (0.5 * gate) * acc_ref[...]


def _ffn_call(x, mod_l, g, wg, wu, wo, mod_row):
    tiles_per_batch = SEQ // TM
    tok = pl.BlockSpec((TM, D_MODEL), lambda i: (i, 0))
    return pl.pallas_call(
        functools.partial(_ffn_kernel, mod_row=mod_row),
        grid=(N_TOK // TM,),
        in_specs=[
            tok,
            pl.BlockSpec((1, N_MOD, D_MODEL), lambda i: (i // tiles_per_batch, 0, 0)),
            pl.BlockSpec((1, D_MODEL), lambda i: (0, 0)),
            _resident(wg.shape), _resident(wu.shape), _resident(wo.shape),
        ],
        out_specs=tok,
        out_shape=jax.ShapeDtypeStruct((N_TOK, D_MODEL), F32),
        scratch_shapes=[pltpu.VMEM((TM, D_MODEL), F32)],
        compiler_params=_params(1),
        name="ffn_swiglu",
    )(x, mod_l, g.reshape(1, D_MODEL), wg, wu, wo)


def _ffn_weights(w_in, w_out):
    wg = w_in[:, :D_FF].reshape(D_MODEL, N_FF_CHUNKS, FF_CHUNK).transpose(1, 0, 2).astype(BF16)
    wu = w_in[:, D_FF:].reshape(D_MODEL, N_FF_CHUNKS, FF_CHUNK).transpose(1, 0, 2).astype(BF16)
    wo = w_out.reshape(N_FF_CHUNKS, FF_CHUNK, D_MODEL).astype(BF16)
    return wg, wu, wo


def _rope(x, cos, sin, lo_mask, shift_up, shift_dn):
    outs = []
    for j in range(x.shape[1] // LANES):
        xj = x[:, j * LANES:(j + 1) * LANES]
        partner = jnp.where(lo_mask, pltpu.roll(xj, shift_up, 1), pltpu.roll(xj, shift_dn, 1))
        outs.append(xj * cos + partner * sin)
    return jnp.concatenate(outs, axis=1)


def _proj_kernel(x_ref, mod_ref, g_ref, w_ref, cd_ref, sd_ref, cr_ref, sr_ref,
                 rx_ref, rg_ref, qt_ref, k_ref, vt_ref, tq_ref, tk_ref, tv_ref, tg_ref):
    x = x_ref[...]
    shift = mod_ref[0, 3:4, :]
    scale = mod_ref[0, 4:5, :]
    h = _norm_mod(x, g_ref[...], shift, scale).astype(BF16)

    def proj(lo, width):
        return jnp.dot(h, w_ref[:, lo:lo + width], preferred_element_type=F32)

    lane = lax.broadcasted_iota(jnp.int32, (TM, LANES), 1)
    lo_d = (lane & (DIFF_SUB - 1)) < (ROPE_DIM // 2)
    lo_r = (lane & (HEAD_DIM - 1)) < (HEAD_DIM // 2)
    cd, sd, cr, sr = cd_ref[...], sd_ref[...], cr_ref[...], sr_ref[...]
    rope_d = lambda v: _rope(v, cd, sd, lo_d, LANES - ROPE_DIM // 2, ROPE_DIM // 2)
    rope_r = lambda v: _rope(v, cr, sr, lo_r, LANES - HEAD_DIM // 2, HEAD_DIM // 2)

    rx_ref[...] = proj(0, REC_WIDTH)
    rg_ref[...] = proj(384, REC_WIDTH)
    q = rope_d(proj(768, DIFF_WIDTH)) * (DIFF_SUB ** -0.5 * math.log2(math.e))
    qt = q.T
    k_ref[...] = rope_d(proj(1152, DIFF_WIDTH)).astype(BF16)
    vt = proj(1536, DIFF_WIDTH).T
    for j in range(TM // ATT_BLK):
        qt_ref[0, j] = qt[:, j * ATT_BLK:(j + 1) * ATT_BLK].astype(BF16)
        vt_ref[0, j] = vt[:, j * ATT_BLK:(j + 1) * ATT_BLK].astype(BF16)
    tq_ref[...] = rope_r(proj(1920, RET_WIDTH)).astype(BF16)
    tk_ref[...] = (rope_r(proj(2176, RET_WIDTH)) * (HEAD_DIM ** -0.5)).astype(BF16)
    tv_ref[...] = proj(2432, RET_WIDTH).astype(BF16)
    tg_ref[...] = proj(2688, RET_WIDTH)


def _proj_call(x, mod_l, g, w, tables):
    tiles_per_batch = SEQ // TM
    blocks_per_tile = TM // ATT_BLK
    n_blk = SEQ // ATT_BLK
    tok = lambda w_, : pl.BlockSpec((TM, w_), lambda i: (i, 0))
    tspec = pl.BlockSpec((1, blocks_per_tile, DIFF_WIDTH, ATT_BLK),
                         lambda i: (i // tiles_per_batch, i % tiles_per_batch, 0, 0))
    tshape = jax.ShapeDtypeStruct((BATCH, n_blk, DIFF_WIDTH, ATT_BLK), BF16)
    tok_shape = lambda w_, dt: jax.ShapeDtypeStruct((N_TOK, w_), dt)
    return pl.pallas_call(
        _proj_kernel,
        grid=(N_TOK // TM,),
        in_specs=[
            tok(D_MODEL),
            pl.BlockSpec((1, N_MOD, D_MODEL), lambda i: (i // tiles_per_batch, 0, 0)),
            pl.BlockSpec((1, D_MODEL), lambda i: (0, 0)),
            _resident(w.shape),
            tok(LANES), tok(LANES), tok(LANES), tok(LANES),
        ],
        out_specs=[tok(REC_WIDTH), tok(REC_WIDTH), tspec, tok(DIFF_WIDTH), tspec,
                   tok(RET_WIDTH), tok(RET_WIDTH), tok(RET_WIDTH), tok(RET_WIDTH)],
        out_shape=[tok_shape(REC_WIDTH, F32), tok_shape(REC_WIDTH, F32), tshape, tok_shape(DIFF_WIDTH, BF16), tshape,
                   tok_shape(RET_WIDTH, BF16), tok_shape(RET_WIDTH, BF16), tok_shape(RET_WIDTH, BF16),
                   tok_shape(RET_WIDTH, F32)],
        compiler_params=_params(1),
        name="mixer_in_proj",
    )(x, mod_l, g.reshape(1, D_MODEL), w, *tables)


def _rglru_kernel(rx_ref, rg_ref, cw_ref, cb_ref, wa_ref, ba_ref, wx_ref, bx_ref, lam_ref,
                  y_ref, xe_ref, h_ref):
    t = pl.program_id(1)

    @pl.when(t == 0)
    def _():
        xe_ref[0:8, :] = jnp.zeros((8, REC_WIDTH), F32)
        h_ref[...] = jnp.zeros_like(h_ref)

    x = rx_ref[...]
    xe_ref[8:8 + LRU_BLK, :] = x
    cw = cw_ref[...]
    xc = (cb_ref[...] + cw[3:4, :] * x
          + cw[2:3, :] * xe_ref[7:7 + LRU_BLK, :]
          + cw[1:2, :] * xe_ref[6:6 + LRU_BLK, :]
          + cw[0:1, :] * xe_ref[5:5 + LRU_BLK, :])
    xe_ref[0:8, :] = x[LRU_BLK - 8:, :]

    xb = xc.astype(BF16)
    r = _sigmoid(jnp.dot(xb, wa_ref[...], preferred_element_type=F32) + ba_ref[...])
    gi = _sigmoid(jnp.dot(xb, wx_ref[...], preferred_element_type=F32) + bx_ref[...])
    nl = -lam_ref[...]
    softplus = jnp.maximum(nl, 0.0) + jnp.log1p(jnp.exp(-jnp.abs(nl)))
    log_a = -REC_C * r * softplus
    a = jnp.exp(log_a)
    b = jnp.sqrt(-jnp.tanh(log_a) * (a * a + 1.0)) * (gi * xc)

    row = lax.broadcasted_iota(jnp.int32, (LRU_BLK, REC_WIDTH), 0)
    s = 1
    while s < LRU_BLK:
        valid = row >= s
        a_prev = pltpu.roll(a, s, 0)
        b_prev = pltpu.roll(b, s, 0)
        b = jnp.where(valid, a * b_prev, 0.0) + b
        a = jnp.where(valid, a * a_prev, a)
        s *= 2
    h = a * h_ref[...] + b
    h_ref[...] = h[LRU_BLK - 1:, :]

    gate = rg_ref[...]
    cdf = 0.5 * (1.0 + jnp.tanh(math.sqrt(2.0 / math.pi) * (gate + 0.044715 * (gate * gate * gate))))
    y_ref[...] = (gate * cdf * h).astype(BF16)


def _rglru_call(rx, rg, conv_w, conv_b, wa_bd, ba, wx_bd, bx, lam):
    nt = SEQ // LRU_BLK
    tok = pl.BlockSpec((LRU_BLK, REC_WIDTH), lambda b, t: (b * nt + t, 0))
    vec = pl.BlockSpec((1, REC_WIDTH), lambda b, t: (0, 0))
    mat = pl.BlockSpec((REC_WIDTH, REC_WIDTH), lambda b, t: (0, 0))
    row = lambda v: v.reshape(1, REC_WIDTH)
    return pl.pallas_call(
        _rglru_kernel,
        grid=(BATCH, nt),
        in_specs=[tok, tok, pl.BlockSpec((4, REC_WIDTH), lambda b, t: (0, 0)), vec, mat, vec, mat, vec, vec],
        out_specs=tok,
        out_shape=jax.ShapeDtypeStruct((N_TOK, REC_WIDTH), BF16),
        scratch_shapes=[pltpu.VMEM((LRU_BLK + 8, REC_WIDTH), F32), pltpu.VMEM((1, REC_WIDTH), F32)],
        compiler_params=_params(2),
        name="rglru",
    )(rx, rg, conv_w, row(conv_b), wa_bd, row(ba), wx_bd, row(bx), row(lam))


def _block_diag(w):
    h, d, _ = w.shape
    eye = jnp.eye(h, dtype=w.dtype)
    return jnp.einsum('hij,hk->hikj', w, eye).reshape(h * d, h * d)


def _diff_kernel(lam_ref, qt_ref, k_ref, vt_ref, g_ref, o_ref,
                 qm_ref, st_ref, mb_ref, p_ref, al_ref, m_ref, l_ref, acc_ref, *, lambda_init):
    i = pl.program_id(2)
    qt = qt_ref[0, 0]
    rowq = lax.broadcasted_iota(jnp.int32, (LANES, ATT_BLK), 0)
    for idx in range(4):
        band = (rowq >= idx * DIFF_SUB) & (rowq < (idx + 1) * DIFF_SUB)
        qm_ref[idx] = jnp.where(band, qt, jnp.zeros_like(qt))
        m_ref[idx] = jnp.full((1, ATT_BLK), NEG_BIG, F32)
        l_ref[idx] = jnp.zeros((1, ATT_BLK), F32)
        acc_ref[idx] = jnp.zeros((HEAD_DIM, ATT_BLK), F32)
    st_ref[1] = jnp.full(st_ref.shape[1:], 2.0 * NEG_BIG, F32)
    mb_ref[1] = jnp.full(mb_ref.shape[1:], 2.0 * NEG_BIG, F32)
    p_ref[0] = jnp.zeros(p_ref.shape[1:], BF16)
    al_ref[0] = jnp.ones(al_ref.shape[1:], F32)

    def scores(t, slot, masked):
        kb = k_ref[0, pl.ds(pl.multiple_of(t * ATT_BLK, ATT_BLK), ATT_BLK), :]
        if masked:
            kk = lax.broadcasted_iota(jnp.int32, (ATT_BLK, ATT_BLK), 0)
            qq = lax.broadcasted_iota(jnp.int32, (ATT_BLK, ATT_BLK), 1)
            causal = kk <= qq
        for idx in range(4):
            st = jnp.dot(kb, qm_ref[idx], preferred_element_type=F32)
            if masked:
                st = jnp.where(causal, st, NEG_BIG)
            st_ref[slot, idx] = st
            mb_ref[slot, idx] = jnp.max(st, axis=0, keepdims=True)

    def softmax(slot):
        for idx in range(4):
            m_old = m_ref[idx]
            m_new = jnp.maximum(m_old, mb_ref[slot, idx])
            alpha = jnp.exp2(m_old - m_new)
            p = jnp.exp2(st_ref[slot, idx] - m_new)
            l_ref[idx] = alpha * l_ref[idx] + jnp.sum(p, axis=0, keepdims=True)
            p_ref[slot, idx] = p.astype(BF16)
            al_ref[slot, idx] = alpha
            m_ref[idx] = m_new

    def values(t, slot):
        vb = vt_ref[0, jnp.maximum(t, 0)]
        for idx in range(4):
            vh = vb[(idx // 2) * HEAD_DIM:(idx // 2 + 1) * HEAD_DIM, :]
            acc_ref[idx] = (al_ref[slot, idx] * acc_ref[idx]
                            + jnp.dot(vh, p_ref[slot, idx], preferred_element_type=F32))

    def advance(t, slot, masked):
        values(t - 2, slot)
        softmax(1 - slot)
        scores(t, slot, masked)

    def drain(last_slot):
        values(i - 1, 1 - last_slot)
        softmax(last_slot)
        values(i, last_slot)

    def pair(u, carry):
        advance(2 * u, 0, False)
        advance(2 * u + 1, 1, False)
        return carry

    lax.fori_loop(0, i // 2, pair, 0)

    @pl.when(i % 2 == 0)
    def _():
        advance(i, 0, True)
        drain(0)

    @pl.when(i % 2 == 1)
    def _():
        advance(i - 1, 0, False)
        advance(i, 1, True)
        drain(1)

    lv = lam_ref[...]
    lam = (jnp.exp(jnp.sum(lv[0:1] * lv[1:2], axis=-1, keepdims=True))
           - jnp.exp(jnp.sum(lv[2:3] * lv[3:4], axis=-1, keepdims=True)) + lambda_init)
    outs = []
    for hh in range(2):
        o = acc_ref[2 * hh] / l_ref[2 * hh] - lam * (acc_ref[2 * hh + 1] / l_ref[2 * hh + 1])
        ms = jnp.mean(o * o, axis=0, keepdims=True)
        outs.append(o * lax.rsqrt(ms + EPS) * g_ref[...] * (1.0 - lambda_init))
    o_ref[...] = jnp.concatenate(outs, axis=0).T.astype(BF16)


def _diff_call(lam_vecs, qt, k, vt, subln_g, lambda_init):
    nq = SEQ // ATT_BLK
    groups = DIFF_WIDTH // LANES
    return pl.pallas_call(
        functools.partial(_diff_kernel, lambda_init=lambda_init),
        grid=(BATCH, groups, nq),
        in_specs=[
            pl.BlockSpec((4, LANES), lambda b, g, i: (0, 0)),
            pl.BlockSpec((1, 1, LANES, ATT_BLK), lambda b, g, i: (b, i, g, 0)),
            pl.BlockSpec((1, SEQ, LANES), lambda b, g, i: (b, 0, g)),
            pl.BlockSpec((1, nq, LANES, ATT_BLK), lambda b, g, i: (b, 0, g, 0)),
            pl.BlockSpec((HEAD_DIM, 1), lambda b, g, i: (0, 0)),
        ],
        out_specs=pl.BlockSpec((ATT_BLK, LANES), lambda b, g, i: (b * nq + i, g)),
        out_shape=jax.ShapeDtypeStruct((N_TOK, DIFF_WIDTH), BF16),
        scratch_shapes=[
            pltpu.VMEM((4, LANES, ATT_BLK), BF16),
            pltpu.VMEM((2, 4, ATT_BLK, ATT_BLK), F32),
            pltpu.VMEM((2, 4, 1, ATT_BLK), F32),
            pltpu.VMEM((2, 4, ATT_BLK, ATT_BLK), BF16),
            pltpu.VMEM((2, 4, 1, ATT_BLK), F32),
            pltpu.VMEM((4, 1, ATT_BLK), F32),
            pltpu.VMEM((4, 1, ATT_BLK), F32),
            pltpu.VMEM((4, HEAD_DIM, ATT_BLK), F32),
        ],
        compiler_params=_params(3),
        name="diff_attn",
    )(lam_vecs, qt, k.reshape(BATCH, SEQ, DIFF_WIDTH), vt, subln_g.reshape(HEAD_DIM, 1))


def _ret_kernel(q_ref, k_ref, v_ref, g_ref, dec_ref, kdec_ref, qdec_ref, cdec_ref, bmask_ref, o_ref, s_ref):
    c = pl.program_id(2)

    @pl.when(c == 0)
    def _():
        s_ref[...] = jnp.zeros_like(s_ref)

    q = q_ref[...]
    k = k_ref[...]
    v = v_ref[...]
    lane = lax.broadcasted_iota(jnp.int32, (RET_BLK, LANES), 1)
    first = lane < HEAD_DIM
    state = s_ref[...]
    o = jnp.dot(q, state.astype(BF16), preferred_element_type=F32) * qdec_ref[0]
    for hh in range(2):
        sel = first if hh == 0 else jnp.logical_not(first)
        qm = jnp.where(sel, q, jnp.zeros_like(q))
        sc = lax.dot_general(qm, k, (((1,), (1,)), ((), ())), preferred_element_type=F32) * dec_ref[0, hh]
        oi = jnp.dot(sc.astype(BF16), v, preferred_element_type=F32)
        o = o + jnp.where(sel, oi, 0.0)
    kd_t = (k.astype(F32) * kdec_ref[0]).T.astype(BF16)
    kv = jnp.dot(kd_t, v, preferred_element_type=F32)
    s_ref[...] = state * cdec_ref[0] + kv * bmask_ref[...]

    o2 = o * o
    ms0 = jnp.sum(jnp.where(first, o2, 0.0), axis=-1, keepdims=True) * (1.0 / HEAD_DIM)
    ms1 = jnp.sum(jnp.where(first, 0.0, o2), axis=-1, keepdims=True) * (1.0 / HEAD_DIM)
    rs = jnp.where(first, lax.rsqrt(ms0 + EPS), lax.rsqrt(ms1 + EPS))
    gate = g_ref[...]
    o_ref[...] = (gate * _sigmoid(gate) * (o * rs)).astype(BF16)


def _ret_consts():
    c = RET_BLK
    lg = jnp.log1p(-jnp.exp2(-5.0 - jnp.arange(RET_HEADS, dtype=F32)))
    idx = jnp.arange(c, dtype=F32)
    rel = idx[:, None] - idx[None, :]
    decay = jnp.where(rel >= 0, jnp.exp(jnp.maximum(rel, 0.0)[None] * lg[:, None, None]), 0.0)
    k_decay = jnp.exp((c - 1.0 - idx)[None, :] * lg[:, None])
    q_decay = jnp.exp((idx + 1.0)[None, :] * lg[:, None])
    chunk_decay = jnp.exp(c * lg)
    pairs = RET_HEADS // 2
    lanes = lambda t: jnp.repeat(t.reshape(pairs, 2, c).transpose(0, 2, 1), HEAD_DIM, axis=2)
    cdec = jnp.broadcast_to(jnp.repeat(chunk_decay.reshape(pairs, 2), HEAD_DIM, axis=1)[:, :, None],
                            (pairs, LANES, LANES))
    blk = jnp.arange(LANES) // HEAD_DIM
    bmask = (blk[:, None] == blk[None, :]).astype(F32)
    return decay.reshape(pairs, 2, c, c), lanes(k_decay), lanes(q_decay), cdec, bmask


def _ret_call(tq, tk, tv, tg, consts):
    decay, kdec, qdec, cdec, bmask = consts
    nc = SEQ // RET_BLK
    tok = pl.BlockSpec((RET_BLK, LANES), lambda b, p, c: (b * nc + c, p))
    return pl.pallas_call(
        _ret_kernel,
        grid=(BATCH, RET_HEADS // 2, nc),
        in_specs=[
            tok, tok, tok, tok,
            pl.BlockSpec((1, 2, RET_BLK, RET_BLK), lambda b, p, c: (p, 0, 0, 0)),
            pl.BlockSpec((1, RET_BLK, LANES), lambda b, p, c: (p, 0, 0)),
            pl.BlockSpec((1, RET_BLK, LANES), lambda b, p, c: (p, 0, 0)),
            pl.BlockSpec((1, LANES, LANES), lambda b, p, c: (p, 0, 0)),
            pl.BlockSpec((LANES, LANES), lambda b, p, c: (0, 0)),
        ],
        out_specs=tok,
        out_shape=jax.ShapeDtypeStruct((N_TOK, RET_WIDTH), BF16),
        scratch_shapes=[pltpu.VMEM((LANES, LANES), F32)],
        compiler_params=_params(3),
        name="retention",
    )(tq, tk, tv, tg, decay, kdec, qdec, cdec, bmask)


def _outproj_kernel(x_ref, mod_ref, yr_ref, yd_ref, yt_ref, w_ref, o_ref):
    gate = mod_ref[0, 5:6, :]
    y = jnp.dot(yr_ref[...], w_ref[0:REC_WIDTH, :], preferred_element_type=F32)
    y = y + jnp.dot(yd_ref[...], w_ref[REC_WIDTH:REC_WIDTH + DIFF_WIDTH, :], preferred_element_type=F32)
    y = y + jnp.dot(yt_ref[...], w_ref[REC_WIDTH + DIFF_WIDTH:, :], preferred_element_type=F32)
    o_ref[...] = x_ref[...] + gate * y


def _outproj_call(x, mod_l, y_rec, y_diff, y_ret, w):
    tiles_per_batch = SEQ // TM
    tok = lambda w_: pl.BlockSpec((TM, w_), lambda i: (i, 0))
    return pl.pallas_call(
        _outproj_kernel,
        grid=(N_TOK // TM,),
        in_specs=[
            tok(D_MODEL),
            pl.BlockSpec((1, N_MOD, D_MODEL), lambda i: (i // tiles_per_batch, 0, 0)),
            tok(REC_WIDTH), tok(DIFF_WIDTH), tok(RET_WIDTH),
            _resident(w.shape),
        ],
        out_specs=tok(D_MODEL),
        out_shape=jax.ShapeDtypeStruct((N_TOK, D_MODEL), F32),
        compiler_params=_params(1),
        name="mixer_out_proj",
    )(x, mod_l, y_rec, y_diff, y_ret, w)


def _final_norm_kernel(x_ref, g_ref, o_ref):
    x = x_ref[...]
    ms = jnp.mean(x * x, axis=-1, keepdims=True)
    o_ref[...] = x * lax.rsqrt(ms + EPS) * g_ref[...]


def _final_norm_call(x, g):
    tok = pl.BlockSpec((TAB_BLK, D_MODEL), lambda i: (i, 0))
    return pl.pallas_call(
        _final_norm_kernel,
        grid=(N_TOK // TAB_BLK,),
        in_specs=[tok, pl.BlockSpec((1, D_MODEL), lambda i: (0, 0))],
        out_specs=tok,
        out_shape=jax.ShapeDtypeStruct((N_TOK, D_MODEL), F32),
        compiler_params=_params(1),
        name="final_norm",
    )(x, g.reshape(1, D_MODEL))


def kernel(x, c, positions, norm_ffn1_g, norm_mix_g, norm_ffn2_g, ada_w, ada_b, ffn1_w_in, ffn1_w_out, ffn2_w_in, ffn2_w_out, w_in, w_out, rec_conv_w, rec_conv_b, rec_gate_a_w, rec_gate_a_b, rec_gate_x_w, rec_gate_x_b, rec_lambda, diff_lambda_q1, diff_lambda_k1, diff_lambda_q2, diff_lambda_k2, diff_subln_g, final_norm_g):
    xf = x.reshape(N_TOK, D_MODEL)
    mod = _ada_call(c, ada_w, ada_b).reshape(DEPTH, BATCH, N_MOD, D_MODEL)
    tables = _tables_call(positions)
    ret_consts = _ret_consts()
    for l in range(DEPTH):
        xf = _ffn_call(xf, mod[l], norm_ffn1_g[l], *_ffn_weights(ffn1_w_in[l], ffn1_w_out[l]), mod_row=0)

        rx, rg, qt, k, vt, tq, tk, tv, tg = _proj_call(xf, mod[l], norm_mix_g[l], w_in[l].astype(BF16), tables)
        y_rec = _rglru_call(rx, rg, rec_conv_w[l], rec_conv_b[l],
                            _block_diag(rec_gate_a_w[l]).astype(BF16), rec_gate_a_b[l],
                            _block_diag(rec_gate_x_w[l]).astype(BF16), rec_gate_x_b[l], rec_lambda[l])
        lambda_init = 0.8 - 0.6 * math.exp(-0.3 * l)
        lam_vecs = jnp.stack([diff_lambda_q1[l], diff_lambda_k1[l], diff_lambda_q2[l], diff_lambda_k2[l]])
        lam_vecs = jnp.pad(lam_vecs, ((0, 0), (0, LANES - DIFF_SUB)))
        y_diff = _diff_call(lam_vecs, qt, k, vt, diff_subln_g[l], lambda_init)
        y_ret = _ret_call(tq, tk, tv, tg, ret_consts)
        xf = _outproj_call(xf, mod[l], y_rec, y_diff, y_ret, w_out[l].astype(BF16))

        xf = _ffn_call(xf, mod[l], norm_ffn2_g[l], *_ffn_weights(ffn2_w_in[l], ffn2_w_out[l]), mod_row=6)
    return _final_norm_call(xf, final_norm_g).reshape(BATCH, SEQ, D_MODEL)
```

```python
import functools
import math

import jax
import jax.numpy as jnp
from jax import lax
from jax.experimental import pallas as pl
from jax.experimental.pallas import tpu as pltpu

F32 = jnp.float32
BF16 = jnp.bfloat16

D_MODEL = 1024
BATCH = 4
SEQ = 4096
DEPTH = 4
N_TOK = BATCH * SEQ
HEAD_DIM = 64
REC_WIDTH = 384
REC_HEADS = 6
REC_CONV = 4
REC_C = 8.0
DIFF_WIDTH = 384
DIFF_HEADS = 6
DIFF_SUB = 32
ROPE_THETA = 500000.0
ROPE_DIM = 8
RET_WIDTH = 256
RET_HEADS = 4
RET_ROT_BASE = 10000.0
IN_WIDTH = 2944
D_FF = 2816
N_MOD = 9
EPS = 1e-6

LANES = 128
SUBLANES = 8
MXU_DIM = 256
VMEM_LIMIT_BYTES = 56 * 1024 * 1024

TM = 512
FF_CHUNK = MXU_DIM
N_FF_CHUNKS = D_FF // FF_CHUNK
ATT_BLK = 256
ATT_ACC_ROWS = HEAD_DIM + 16
ATT_UNROLL = 4
RET_BLK = 256
RET_TILE = 1024
LRU_BLK = 512
TAB_BLK = 1024
NEG_BIG = -1e30


def _sigmoid(x):
    return 1.0 / (1.0 + jnp.exp(-x))


def _norm_mod(x, g, shift, scale):
    ms = jnp.mean(x * x, axis=-1, keepdims=True)
    y = x * lax.rsqrt(ms + EPS) * g
    return y * (1.0 + scale) + shift


def _resident(shape):
    nd = len(shape)
    return pl.BlockSpec(shape, lambda *_: (0,) * nd, pipeline_mode=pl.Buffered(1))


def _params(n_axes, vmem=VMEM_LIMIT_BYTES, flags=None):
    return pltpu.CompilerParams(dimension_semantics=("arbitrary",) * n_axes, vmem_limit_bytes=vmem, flags=flags)


def _ada_kernel(c_ref, w_ref, b_ref, o_ref):
    c = c_ref[...]
    s = c * _sigmoid(c)
    o_ref[0] = jnp.dot(s, w_ref[0], preferred_element_type=F32, precision=lax.Precision.HIGHEST) + b_ref[0]


def _ada_call(c, ada_w, ada_b):
    tn = 1536
    nmod = N_MOD * D_MODEL
    return pl.pallas_call(
        _ada_kernel,
        grid=(DEPTH, nmod // tn),
        in_specs=[
            pl.BlockSpec((BATCH, D_MODEL), lambda l, j: (0, 0)),
            pl.BlockSpec((1, D_MODEL, tn), lambda l, j: (l, 0, j)),
            pl.BlockSpec((1, 1, tn), lambda l, j: (l, 0, j)),
        ],
        out_specs=pl.BlockSpec((1, BATCH, tn), lambda l, j: (l, 0, j)),
        out_shape=jax.ShapeDtypeStruct((DEPTH, BATCH, nmod), F32),
        compiler_params=_params(2),
        name="ada_mod",
    )(c, ada_w, ada_b.reshape(DEPTH, 1, nmod))


def _tables_kernel(pos_ref, invd_ref, sgnd_ref, invr_ref, sgnr_ref, cd_ref, sd_ref, cr_ref, sr_ref):
    pos = pos_ref[...].astype(F32)
    ang_d = pos * invd_ref[...]
    cd_ref[...] = jnp.cos(ang_d)
    sd_ref[...] = jnp.sin(ang_d) * sgnd_ref[...]
    ang_r = pos * invr_ref[...]
    cr_ref[...] = jnp.cos(ang_r)
    sr_ref[...] = jnp.sin(ang_r) * sgnr_ref[...]


def _rotary_consts():
    lane = jnp.arange(LANES)
    half_d = ROPE_DIM // 2
    inv_d = jnp.power(jnp.float32(ROPE_THETA), -jnp.arange(half_d, dtype=F32) * (2.0 / ROPE_DIM))
    cd = lane % DIFF_SUB
    inv_d_lane = jnp.where(cd < ROPE_DIM, inv_d[cd % half_d], 0.0)
    sgn_d_lane = jnp.where(cd < half_d, -1.0, jnp.where(cd < ROPE_DIM, 1.0, 0.0))
    half_r = HEAD_DIM // 2
    inv_r = jnp.power(jnp.float32(RET_ROT_BASE), -jnp.arange(half_r, dtype=F32) * (2.0 / HEAD_DIM))
    cr = lane % HEAD_DIM
    inv_r_lane = inv_r[cr % half_r]
    sgn_r_lane = jnp.where(cr < half_r, -1.0, 1.0)
    row = lambda v: v.astype(F32).reshape(1, LANES)
    return row(inv_d_lane), row(sgn_d_lane), row(inv_r_lane), row(sgn_r_lane)


def _tables_call(positions):
    consts = _rotary_consts()
    vec = pl.BlockSpec((1, LANES), lambda i: (0, 0))
    tab = pl.BlockSpec((TAB_BLK, LANES), lambda i: (i, 0))
    return pl.pallas_call(
        _tables_kernel,
        grid=(N_TOK // TAB_BLK,),
        in_specs=[pl.BlockSpec((TAB_BLK, 1), lambda i: (i, 0)), vec, vec, vec, vec],
        out_specs=[tab, tab, tab, tab],
        out_shape=[jax.ShapeDtypeStruct((N_TOK, LANES), F32)] * 4,
        compiler_params=_params(1),
        name="rotary_tables",
    )(positions.reshape(N_TOK, 1), *consts)


def _ffn_kernel(x_ref, mod_ref, g_ref, wi_ref, wo_ref, o_ref, acc_ref, *, mod_row):
    x = x_ref[...]
    shift = mod_ref[0, mod_row:mod_row + 1, :]
    scale = mod_ref[0, mod_row + 1:mod_row + 2, :]
    gate = mod_ref[0, mod_row + 2:mod_row + 3, :]
    h = _norm_mod(x, g_ref[...], shift, scale).astype(BF16)
    for c in range(N_FF_CHUNKS):
        lo = c * FF_CHUNK
        hg = jnp.dot(h, wi_ref[:, lo:lo + FF_CHUNK], preferred_element_type=F32)
        hu = jnp.dot(h, wi_ref[:, D_FF + lo:D_FF + lo + FF_CHUNK], preferred_element_type=F32)
        a = (hg * _sigmoid(hg) * hu).astype(BF16)
        part = jnp.dot(a, wo_ref[lo:lo + FF_CHUNK, :], preferred_element_type=F32)
        if c == 0:
            acc_ref[...] = part
        else:
            acc_ref[...] += part
    o_ref[...] = x + (0.5 * gate) * acc_ref[...]


def _ffn_call(x, mod_l, g, w_in, w_out, mod_row):
    tiles_per_batch = SEQ // TM
    tok = pl.BlockSpec((TM, D_MODEL), lambda i: (i, 0))
    return pl.pallas_call(
        functools.partial(_ffn_kernel, mod_row=mod_row),
        grid=(N_TOK // TM,),
        in_specs=[
            tok,
            pl.BlockSpec((1, N_MOD, D_MODEL), lambda i: (i // tiles_per_batch, 0, 0)),
            pl.BlockSpec((1, D_MODEL), lambda i: (0, 0)),
            _resident(w_in.shape), _resident(w_out.shape),
        ],
        out_specs=tok,
        out_shape=jax.ShapeDtypeStruct((N_TOK, D_MODEL), F32),
        scratch_shapes=[pltpu.VMEM((TM, D_MODEL), F32)],
        compiler_params=_params(1),
        name="ffn_swiglu",
    )(x, mod_l, g.reshape(1, D_MODEL), w_in, w_out)


def _rope(x, cos, sin, lo_mask, shift_up, shift_dn):
    outs = []
    for j in range(x.shape[1] // LANES):
        xj = x[:, j * LANES:(j + 1) * LANES]
        partner = jnp.where(lo_mask, pltpu.roll(xj, shift_up, 1), pltpu.roll(xj, shift_dn, 1))
        outs.append(xj * cos + partner * sin)
    return jnp.concatenate(outs, axis=1)


def _proj_kernel(x_ref, mod_ref, g_ref, w_ref, cd_ref, sd_ref, cr_ref, sr_ref,
                 rx_ref, rg_ref, qt_ref, k_ref, vt_ref, tq_ref, tk_ref, tv_ref, tg_ref):
    x = x_ref[...]
    shift = mod_ref[0, 3:4, :]
    scale = mod_ref[0, 4:5, :]
    h = _norm_mod(x, g_ref[...], shift, scale).astype(BF16)

    groups = {}
    for lo, hi in ((0, 768), (768, 1536), (1536, IN_WIDTH)):
        groups[lo] = jnp.dot(h, w_ref[:, lo:hi], preferred_element_type=F32)

    def proj(lo, width):
        base = max(b for b in groups if b <= lo)
        return groups[base][:, lo - base:lo - base + width]

    lane = lax.broadcasted_iota(jnp.int32, (TM, LANES), 1)
    lo_d = (lane & (DIFF_SUB - 1)) < (ROPE_DIM // 2)
    lo_r = (lane & (HEAD_DIM - 1)) < (HEAD_DIM // 2)
    cd, sd, cr, sr = cd_ref[...], sd_ref[...], cr_ref[...], sr_ref[...]
    rope_d = lambda v: _rope(v, cd, sd, lo_d, LANES - ROPE_DIM // 2, ROPE_DIM // 2)
    rope_r = lambda v: _rope(v, cr, sr, lo_r, LANES - HEAD_DIM // 2, HEAD_DIM // 2)

    rx_ref[...] = proj(0, REC_WIDTH)
    rg_ref[...] = proj(384, REC_WIDTH)
    q = rope_d(proj(768, DIFF_WIDTH)) * (DIFF_SUB ** -0.5 * math.log2(math.e))
    qt = q.T
    k_ref[...] = rope_d(proj(1152, DIFF_WIDTH)).astype(BF16)
    vt = proj(1536, DIFF_WIDTH).T
    for j in range(TM // ATT_BLK):
        qt_ref[0, j] = qt[:, j * ATT_BLK:(j + 1) * ATT_BLK].astype(BF16)
        vt_ref[0, j] = vt[:, j * ATT_BLK:(j + 1) * ATT_BLK].astype(BF16)
    tq_ref[...] = rope_r(proj(1920, RET_WIDTH)).astype(BF16)
    tk_ref[...] = (rope_r(proj(2176, RET_WIDTH)) * (HEAD_DIM ** -0.5)).astype(BF16)
    tv_ref[...] = proj(2432, RET_WIDTH).astype(BF16)
    tg_ref[...] = proj(2688, RET_WIDTH)


def _proj_call(x, mod_l, g, w, tables):
    tiles_per_batch = SEQ // TM
    blocks_per_tile = TM // ATT_BLK
    n_blk = SEQ // ATT_BLK
    tok = lambda w_, : pl.BlockSpec((TM, w_), lambda i: (i, 0))
    tspec = pl.BlockSpec((1, blocks_per_tile, DIFF_WIDTH, ATT_BLK),
                         lambda i: (i // tiles_per_batch, i % tiles_per_batch, 0, 0))
    tshape = jax.ShapeDtypeStruct((BATCH, n_blk, DIFF_WIDTH, ATT_BLK), BF16)
    tok_shape = lambda w_, dt: jax.ShapeDtypeStruct((N_TOK, w_), dt)
    return pl.pallas_call(
        _proj_kernel,
        grid=(N_TOK // TM,),
        in_specs=[
            tok(D_MODEL),
            pl.BlockSpec((1, N_MOD, D_MODEL), lambda i: (i // tiles_per_batch, 0, 0)),
            pl.BlockSpec((1, D_MODEL), lambda i: (0, 0)),
            _resident(w.shape),
            tok(LANES), tok(LANES), tok(LANES), tok(LANES),
        ],
        out_specs=[tok(REC_WIDTH), tok(REC_WIDTH), tspec, tok(DIFF_WIDTH), tspec,
                   tok(RET_WIDTH), tok(RET_WIDTH), tok(RET_WIDTH), tok(RET_WIDTH)],
        out_shape=[tok_shape(REC_WIDTH, F32), tok_shape(REC_WIDTH, F32), tshape, tok_shape(DIFF_WIDTH, BF16), tshape,
                   tok_shape(RET_WIDTH, BF16), tok_shape(RET_WIDTH, BF16), tok_shape(RET_WIDTH, BF16),
                   tok_shape(RET_WIDTH, F32)],
        compiler_params=_params(1),
        name="mixer_in_proj",
    )(x, mod_l, g.reshape(1, D_MODEL), w, *tables)


def _rglru_kernel(rx_ref, rg_ref, cw_ref, cb_ref, wa_ref, ba_ref, wx_ref, bx_ref, lam_ref,
                  y_ref, xe_ref, h_ref):
    @pl.when(pl.program_id(1) == 0)
    def _():
        xe_ref[...] = jnp.zeros_like(xe_ref)
        h_ref[...] = jnp.zeros_like(h_ref)

    n_grp = LRU_BLK // SUBLANES
    x = rx_ref[...]
    x3 = x.reshape(n_grp, SUBLANES, REC_WIDTH)
    tail = xe_ref[...]
    within = lax.broadcasted_iota(jnp.int32, (n_grp, SUBLANES, REC_WIDTH), 1)
    cw = cw_ref[...]
    xc3 = cb_ref[...] + cw[REC_CONV - 1:REC_CONV, :] * x3
    for k in range(1, REC_CONV):
        rot = pltpu.roll(x3, k, 1)
        before = jnp.concatenate([pltpu.roll(tail, k, 0)[None], rot[:-1]], axis=0)
        xc3 = xc3 + cw[REC_CONV - 1 - k:REC_CONV - k, :] * jnp.where(within >= k, rot, before)
    xe_ref[...] = x[LRU_BLK - SUBLANES:, :]
    xc = xc3.reshape(LRU_BLK, REC_WIDTH)

    xb = xc.astype(BF16)
    r = _sigmoid(jnp.dot(xb, wa_ref[...], preferred_element_type=F32) + ba_ref[...])
    gi = _sigmoid(jnp.dot(xb, wx_ref[...], preferred_element_type=F32) + bx_ref[...])
    nl = -lam_ref[...]
    softplus = jnp.maximum(nl, 0.0) + jnp.log1p(jnp.exp(-jnp.abs(nl)))
    log_a = -REC_C * r * softplus
    a = jnp.exp(log_a)
    b = jnp.sqrt(-jnp.tanh(log_a) * (a * a + 1.0)) * (gi * xc)

    a = a.reshape(n_grp, SUBLANES, REC_WIDTH)
    b = b.reshape(n_grp, SUBLANES, REC_WIDTH)
    s = 1
    while s < SUBLANES:
        valid = within >= s
        a_prev = pltpu.roll(a, s, 1)
        b_prev = pltpu.roll(b, s, 1)
        b = jnp.where(valid, a * b_prev, 0.0) + b
        a = jnp.where(valid, a * a_prev, a)
        s *= 2
    carry = h_ref[...]
    groups = []
    for grp in range(n_grp):
        hg = a[grp] * carry + b[grp]
        groups.append(hg)
        carry = hg[SUBLANES - 1:, :]
    h = jnp.concatenate(groups, axis=0)
    h_ref[...] = carry

    gate = rg_ref[...]
    cdf = 0.5 * (1.0 + jnp.tanh(math.sqrt(2.0 / math.pi) * (gate + 0.044715 * (gate * gate * gate))))
    y_ref[...] = (gate * cdf * h).astype(BF16)


def _rglru_call(rx, rg, conv_w, conv_b, wa_bd, ba, wx_bd, bx, lam):
    nt = SEQ // LRU_BLK
    tok = pl.BlockSpec((LRU_BLK, REC_WIDTH), lambda b, t: (b * nt + t, 0))
    vec = pl.BlockSpec((1, REC_WIDTH), lambda b, t: (0, 0))
    mat = pl.BlockSpec((REC_WIDTH, REC_WIDTH), lambda b, t: (0, 0))
    row = lambda v: v.reshape(1, REC_WIDTH)
    return pl.pallas_call(
        _rglru_kernel,
        grid=(BATCH, nt),
        in_specs=[tok, tok, pl.BlockSpec((REC_CONV, REC_WIDTH), lambda b, t: (0, 0)), vec, mat, vec, mat, vec, vec],
        out_specs=tok,
        out_shape=jax.ShapeDtypeStruct((N_TOK, REC_WIDTH), BF16),
        scratch_shapes=[pltpu.VMEM((SUBLANES, REC_WIDTH), F32), pltpu.VMEM((1, REC_WIDTH), F32)],
        compiler_params=_params(2),
        name="rglru",
    )(rx, rg, conv_w, row(conv_b), wa_bd, row(ba), wx_bd, row(bx), row(lam))


def _block_diag(w):
    h, d, _ = w.shape
    eye = jnp.eye(h, dtype=w.dtype)
    return jnp.einsum('hij,hk->hikj', w, eye).reshape(h * d, h * d)


def _diff_kernel(lam_ref, qt_ref, k_ref, vt_ref, g_ref, o_ref,
                 qm_ref, st_ref, mb_ref, p_ref, al_ref, m_ref, acc_ref, *, lambda_init):
    i = pl.program_id(2)
    qt = qt_ref[0, 0]
    rowq = lax.broadcasted_iota(jnp.int32, (LANES, ATT_BLK), 0)
    for idx in range(4):
        band = (rowq >= idx * DIFF_SUB) & (rowq < (idx + 1) * DIFF_SUB)
        qm_ref[idx] = jnp.where(band, qt, jnp.zeros_like(qt))
        m_ref[idx] = jnp.full((1, ATT_BLK), NEG_BIG, F32)
        acc_ref[idx] = jnp.zeros((ATT_ACC_ROWS, ATT_BLK), F32)
    st_ref[1] = jnp.full(st_ref.shape[1:], 2.0 * NEG_BIG, F32)
    mb_ref[1] = jnp.full(mb_ref.shape[1:], 2.0 * NEG_BIG, F32)
    p_ref[0] = jnp.zeros(p_ref.shape[1:], BF16)
    al_ref[0] = jnp.ones(al_ref.shape[1:], F32)
    ones_rows = jnp.ones((ATT_ACC_ROWS - HEAD_DIM, ATT_BLK), BF16)

    def scores(t, slot, masked):
        kb = k_ref[0, pl.ds(pl.multiple_of(t * ATT_BLK, ATT_BLK), ATT_BLK), :]
        if masked:
            kk = lax.broadcasted_iota(jnp.int32, (ATT_BLK, ATT_BLK), 0)
            qq = lax.broadcasted_iota(jnp.int32, (ATT_BLK, ATT_BLK), 1)
            causal = kk <= qq
        for idx in range(4):
            st = jnp.dot(kb, qm_ref[idx], preferred_element_type=F32)
            if masked:
                st = jnp.where(causal, st, NEG_BIG)
            st_ref[slot, idx] = st
            mb_ref[slot, idx] = jnp.max(st, axis=0, keepdims=True)

    def softmax(slot):
        for idx in range(4):
            m_old = m_ref[idx]
            m_new = jnp.maximum(m_old, mb_ref[slot, idx])
            p_ref[slot, idx] = jnp.exp2(st_ref[slot, idx] - m_new).astype(BF16)
            al_ref[slot, idx] = jnp.exp2(m_old - m_new)
            m_ref[idx] = m_new

    def values(t, slot):
        vb = vt_ref[0, jnp.maximum(t, 0)]
        for hh in range(2):
            vh = jnp.concatenate([vb[hh * HEAD_DIM:(hh + 1) * HEAD_DIM, :], ones_rows], axis=0)
            for idx in (2 * hh, 2 * hh + 1):
                acc_ref[idx] = (al_ref[slot, idx] * acc_ref[idx]
                                + jnp.dot(vh, p_ref[slot, idx], preferred_element_type=F32))

    def advance(t, slot, masked):
        values(t - 2, slot)
        softmax(1 - slot)
        scores(t, slot, masked)

    def drain(last_slot):
        values(i - 1, 1 - last_slot)
        softmax(last_slot)
        values(i, last_slot)

    def group(u, carry):
        for j in range(ATT_UNROLL):
            advance(ATT_UNROLL * u + j, j % 2, False)
        return carry

    lax.fori_loop(0, i // ATT_UNROLL, group, 0)

    for rem in range(ATT_UNROLL):
        @pl.when(i % ATT_UNROLL == rem)
        def _(rem=rem):
            for j in range(rem):
                advance(i - rem + j, j % 2, False)
            advance(i, rem % 2, True)
            drain(rem % 2)

    lv = lam_ref[...]
    lam = (jnp.exp(jnp.sum(lv[0:1] * lv[1:2], axis=-1, keepdims=True))
           - jnp.exp(jnp.sum(lv[2:3] * lv[3:4], axis=-1, keepdims=True)) + lambda_init)
    outs = []
    for hh in range(2):
        a0, a1 = acc_ref[2 * hh], acc_ref[2 * hh + 1]
        o = (a0[:HEAD_DIM] / a0[HEAD_DIM:HEAD_DIM + 1]
             - lam * (a1[:HEAD_DIM] / a1[HEAD_DIM:HEAD_DIM + 1]))
        ms = jnp.mean(o * o, axis=0, keepdims=True)
        outs.append(o * lax.rsqrt(ms + EPS) * g_ref[...] * (1.0 - lambda_init))
    o_ref[...] = jnp.concatenate(outs, axis=0).T.astype(BF16)


def _diff_call(lam_vecs, qt, k, vt, subln_g, lambda_init):
    nq = SEQ // ATT_BLK
    groups = DIFF_WIDTH // LANES
    return pl.pallas_call(
        functools.partial(_diff_kernel, lambda_init=lambda_init),
        grid=(BATCH, groups, nq),
        in_specs=[
            pl.BlockSpec((4, LANES), lambda b, g, i: (0, 0)),
            pl.BlockSpec((1, 1, LANES, ATT_BLK), lambda b, g, i: (b, i, g, 0)),
            pl.BlockSpec((1, SEQ, LANES), lambda b, g, i: (b, 0, g)),
            pl.BlockSpec((1, nq, LANES, ATT_BLK), lambda b, g, i: (b, 0, g, 0)),
            pl.BlockSpec((HEAD_DIM, 1), lambda b, g, i: (0, 0)),
        ],
        out_specs=pl.BlockSpec((ATT_BLK, LANES), lambda b, g, i: (b * nq + i, g)),
        out_shape=jax.ShapeDtypeStruct((N_TOK, DIFF_WIDTH), BF16),
        scratch_shapes=[
            pltpu.VMEM((4, LANES, ATT_BLK), BF16),
            pltpu.VMEM((2, 4, ATT_BLK, ATT_BLK), F32),
            pltpu.VMEM((2, 4, 1, ATT_BLK), F32),
            pltpu.VMEM((2, 4, ATT_BLK, ATT_BLK), BF16),
            pltpu.VMEM((2, 4, 1, ATT_BLK), F32),
            pltpu.VMEM((4, 1, ATT_BLK), F32),
            pltpu.VMEM((4, ATT_ACC_ROWS, ATT_BLK), F32),
        ],
        compiler_params=_params(3),
        name="diff_attn",
    )(lam_vecs, qt, k.reshape(BATCH, SEQ, DIFF_WIDTH), vt, subln_g.reshape(HEAD_DIM, 1))


def _ret_kernel(q_ref, k_ref, v_ref, g_ref, dec_ref, kdec_ref, qdec_ref, cdec_ref, bmask_ref, o_ref, s_ref):
    @pl.when(pl.program_id(1) == 0)
    def _():
        s_ref[...] = jnp.zeros_like(s_ref)

    lane = lax.broadcasted_iota(jnp.int32, (RET_BLK, LANES), 1)
    first = lane < HEAD_DIM
    for pair in range(RET_HEADS // 2):
        lanes = slice(pair * LANES, (pair + 1) * LANES)
        state = s_ref[pair]
        for chunk in range(RET_TILE // RET_BLK):
            rows = slice(chunk * RET_BLK, (chunk + 1) * RET_BLK)
            q = q_ref[rows, lanes]
            k = k_ref[rows, lanes]
            v = v_ref[rows, lanes]
            o = jnp.dot(q, state.astype(BF16), preferred_element_type=F32) * qdec_ref[pair]
            for hh in range(2):
                sel = first if hh == 0 else jnp.logical_not(first)
                qm = jnp.where(sel, q, jnp.zeros_like(q))
                sc = lax.dot_general(qm, k, (((1,), (1,)), ((), ())),
                                     preferred_element_type=F32) * dec_ref[pair, hh]
                oi = jnp.dot(sc.astype(BF16), v, preferred_element_type=F32)
                o = o + jnp.where(sel, oi, 0.0)
            kd_t = (k.astype(F32) * kdec_ref[pair]).T.astype(BF16)
            kv = jnp.dot(kd_t, v, preferred_element_type=F32)
            state = state * cdec_ref[pair] + kv * bmask_ref[...]

            o2 = o * o
            ms0 = jnp.sum(jnp.where(first, o2, 0.0), axis=-1, keepdims=True) * (1.0 / HEAD_DIM)
            ms1 = jnp.sum(jnp.where(first, 0.0, o2), axis=-1, keepdims=True) * (1.0 / HEAD_DIM)
            rs = jnp.where(first, lax.rsqrt(ms0 + EPS), lax.rsqrt(ms1 + EPS))
            gate = g_ref[rows, lanes]
            o_ref[rows, lanes] = (gate * _sigmoid(gate) * (o * rs)).astype(BF16)
        s_ref[pair] = state


def _ret_consts():
    c = RET_BLK
    lg = jnp.log1p(-jnp.exp2(-5.0 - jnp.arange(RET_HEADS, dtype=F32)))
    idx = jnp.arange(c, dtype=F32)
    rel = idx[:, None] - idx[None, :]
    decay = jnp.where(rel >= 0, jnp.exp(jnp.maximum(rel, 0.0)[None] * lg[:, None, None]), 0.0)
    k_decay = jnp.exp((c - 1.0 - idx)[None, :] * lg[:, None])
    q_decay = jnp.exp((idx + 1.0)[None, :] * lg[:, None])
    chunk_decay = jnp.exp(c * lg)
    pairs = RET_HEADS // 2
    lanes = lambda t: jnp.repeat(t.reshape(pairs, 2, c).transpose(0, 2, 1), HEAD_DIM, axis=2)
    cdec = jnp.broadcast_to(jnp.repeat(chunk_decay.reshape(pairs, 2), HEAD_DIM, axis=1)[:, :, None],
                            (pairs, LANES, LANES))
    blk = jnp.arange(LANES) // HEAD_DIM
    bmask = (blk[:, None] == blk[None, :]).astype(F32)
    return decay.reshape(pairs, 2, c, c), lanes(k_decay), lanes(q_decay), cdec, bmask


def _ret_call(tq, tk, tv, tg, consts):
    decay, kdec, qdec, cdec, bmask = consts
    nt = SEQ // RET_TILE
    tok = pl.BlockSpec((RET_TILE, RET_WIDTH), lambda b, t: (b * nt + t, 0))
    whole = lambda a: pl.BlockSpec(a.shape, lambda b, t: (0,) * a.ndim)
    return pl.pallas_call(
        _ret_kernel,
        grid=(BATCH, nt),
        in_specs=[tok, tok, tok, tok, whole(decay), whole(kdec), whole(qdec), whole(cdec), whole(bmask)],
        out_specs=tok,
        out_shape=jax.ShapeDtypeStruct((N_TOK, RET_WIDTH), BF16),
        scratch_shapes=[pltpu.VMEM((RET_HEADS // 2, LANES, LANES), F32)],
        compiler_params=_params(2),
        name="retention",
    )(tq, tk, tv, tg, decay, kdec, qdec, cdec, bmask)


def _outproj_kernel(x_ref, mod_ref, yr_ref, yd_ref, yt_ref, w_ref, o_ref):
    gate = mod_ref[0, 5:6, :]
    y = jnp.dot(yr_ref[...], w_ref[0:REC_WIDTH, :], preferred_element_type=F32)
    y = y + jnp.dot(yd_ref[...], w_ref[REC_WIDTH:REC_WIDTH + DIFF_WIDTH, :], preferred_element_type=F32)
    y = y + jnp.dot(yt_ref[...], w_ref[REC_WIDTH + DIFF_WIDTH:, :], preferred_element_type=F32)
    o_ref[...] = x_ref[...] + gate * y


def _outproj_call(x, mod_l, y_rec, y_diff, y_ret, w):
    tiles_per_batch = SEQ // TM
    tok = lambda w_: pl.BlockSpec((TM, w_), lambda i: (i, 0))
    return pl.pallas_call(
        _outproj_kernel,
        grid=(N_TOK // TM,),
        in_specs=[
            tok(D_MODEL),
            pl.BlockSpec((1, N_MOD, D_MODEL), lambda i: (i // tiles_per_batch, 0, 0)),
            tok(REC_WIDTH), tok(DIFF_WIDTH), tok(RET_WIDTH),
            _resident(w.shape),
        ],
        out_specs=tok(D_MODEL),
        out_shape=jax.ShapeDtypeStruct((N_TOK, D_MODEL), F32),
        compiler_params=_params(1),
        name="mixer_out_proj",
    )(x, mod_l, y_rec, y_diff, y_ret, w)


def _final_norm_kernel(x_ref, g_ref, o_ref):
    x = x_ref[...]
    ms = jnp.mean(x * x, axis=-1, keepdims=True)
    o_ref[...] = x * lax.rsqrt(ms + EPS) * g_ref[...]


def _final_norm_call(x, g):
    tok = pl.BlockSpec((TAB_BLK, D_MODEL), lambda i: (i, 0))
    return pl.pallas_call(
        _final_norm_kernel,
        grid=(N_TOK // TAB_BLK,),
        in_specs=[tok, pl.BlockSpec((1, D_MODEL), lambda i: (0, 0))],
        out_specs=tok,
        out_shape=jax.ShapeDtypeStruct((N_TOK, D_MODEL), F32),
        compiler_params=_params(1),
        name="final_norm",
    )(x, g.reshape(1, D_MODEL))


def kernel(x, c, positions, norm_ffn1_g, norm_mix_g, norm_ffn2_g, ada_w, ada_b, ffn1_w_in, ffn1_w_out, ffn2_w_in, ffn2_w_out, w_in, w_out, rec_conv_w, rec_conv_b, rec_gate_a_w, rec_gate_a_b, rec_gate_x_w, rec_gate_x_b, rec_lambda, diff_lambda_q1, diff_lambda_k1, diff_lambda_q2, diff_lambda_k2, diff_subln_g, final_norm_g):
    xf = x.reshape(N_TOK, D_MODEL)
    mod = _ada_call(c, ada_w, ada_b).reshape(DEPTH, BATCH, N_MOD, D_MODEL)
    tables = _tables_call(positions)
    ret_consts = _ret_consts()
    for l in range(DEPTH):
        xf = _ffn_call(xf, mod[l], norm_ffn1_g[l], ffn1_w_in[l].astype(BF16), ffn1_w_out[l].astype(BF16), mod_row=0)

        rx, rg, qt, k, vt, tq, tk, tv, tg = _proj_call(xf, mod[l], norm_mix_g[l], w_in[l].astype(BF16), tables)
        y_rec = _rglru_call(rx, rg, rec_conv_w[l], rec_conv_b[l],
                            _block_diag(rec_gate_a_w[l]).astype(BF16), rec_gate_a_b[l],
                            _block_diag(rec_gate_x_w[l]).astype(BF16), rec_gate_x_b[l], rec_lambda[l])
        lambda_init = 0.8 - 0.6 * math.exp(-0.3 * l)
        lam_vecs = jnp.stack([diff_lambda_q1[l], diff_lambda_k1[l], diff_lambda_q2[l], diff_lambda_k2[l]])
        lam_vecs = jnp.pad(lam_vecs, ((0, 0), (0, LANES - DIFF_SUB)))
        y_diff = _diff_call(lam_vecs, qt, k, vt, diff_subln_g[l], lambda_init)
        y_ret = _ret_call(tq, tk, tv, tg, ret_consts)
        xf = _outproj_call(xf, mod[l], y_rec, y_diff, y_ret, w_out[l].astype(BF16))

        xf = _ffn_call(xf, mod[l], norm_ffn2_g[l], ffn2_w_in[l].astype(BF16), ffn2_w_out[l].astype(BF16), mod_row=6)
    return _final_norm_call(xf, final_norm_g).reshape(BATCH, SEQ, D_MODEL)
```

```python
import functools
import math

import jax
import jax.numpy as jnp
from jax import lax
from jax.experimental import pallas as pl
from jax.experimental.pallas import tpu as pltpu

F32 = jnp.float32
BF16 = jnp.bfloat16

D_MODEL = 1024
BATCH = 4
SEQ = 4096
DEPTH = 4
N_TOK = BATCH * SEQ
HEAD_DIM = 64
REC_WIDTH = 384
REC_HEADS = 6
REC_CONV = 4
REC_C = 8.0
DIFF_WIDTH = 384
DIFF_HEADS = 6
DIFF_SUB = 32
ROPE_THETA = 500000.0
ROPE_DIM = 8
RET_WIDTH = 256
RET_HEADS = 4
RET_ROT_BASE = 10000.0
IN_WIDTH = 2944
D_FF = 2816
N_MOD = 9
EPS = 1e-6

LANES = 128
SUBLANES = 8
MXU_DIM = 256
VMEM_LIMIT_BYTES = 56 * 1024 * 1024

TM = 512
TM_FFN = 1024
FF_CHUNK = MXU_DIM
N_FF_CHUNKS = D_FF // FF_CHUNK
ATT_BLK = 256
ATT_ACC_ROWS = HEAD_DIM + 16
ATT_UNROLL = 4
ATT_GROUPS = 1
RET_BLK = 256
RET_TILE = 1024
LRU_BLK = 512
TAB_BLK = 1024
NEG_BIG = -1e30


def _sigmoid(x):
    return 1.0 / (1.0 + jnp.exp(-x))


def _norm_mod(x, g, shift, scale):
    ms = jnp.mean(x * x, axis=-1, keepdims=True)
    y = x * lax.rsqrt(ms + EPS) * g
    return y * (1.0 + scale) + shift


def _resident_layer(stacked_shape, layer):
    nd = len(stacked_shape) - 1
    return pl.BlockSpec((None,) + tuple(stacked_shape[1:]), lambda *_: (layer,) + (0,) * nd,
                        pipeline_mode=pl.Buffered(1))


def _mod_spec(layer, rows_per_step):
    steps_per_batch = SEQ // rows_per_step
    return pl.BlockSpec((None, 1, N_MOD, D_MODEL), lambda i: (layer, i // steps_per_batch, 0, 0))


def _gain_spec(layer):
    return pl.BlockSpec((None, 1, D_MODEL), lambda i: (layer, 0, 0))


def _params(n_axes, vmem=VMEM_LIMIT_BYTES, flags=None):
    return pltpu.CompilerParams(dimension_semantics=("arbitrary",) * n_axes, vmem_limit_bytes=vmem, flags=flags)


def _ada_kernel(c_ref, w_ref, b_ref, o_ref):
    c = c_ref[...]
    s = c * _sigmoid(c)
    o_ref[0] = jnp.dot(s, w_ref[0], preferred_element_type=F32, precision=lax.Precision.HIGHEST) + b_ref[0]


def _ada_call(c, ada_w, ada_b):
    tn = 1536
    nmod = N_MOD * D_MODEL
    return pl.pallas_call(
        _ada_kernel,
        grid=(DEPTH, nmod // tn),
        in_specs=[
            pl.BlockSpec((BATCH, D_MODEL), lambda l, j: (0, 0)),
            pl.BlockSpec((1, D_MODEL, tn), lambda l, j: (l, 0, j)),
            pl.BlockSpec((1, 1, tn), lambda l, j: (l, 0, j)),
        ],
        out_specs=pl.BlockSpec((1, BATCH, tn), lambda l, j: (l, 0, j)),
        out_shape=jax.ShapeDtypeStruct((DEPTH, BATCH, nmod), F32),
        compiler_params=_params(2),
        name="ada_mod",
    )(c, ada_w, ada_b.reshape(DEPTH, 1, nmod))


def _tables_kernel(pos_ref, invd_ref, sgnd_ref, invr_ref, sgnr_ref, cd_ref, sd_ref, cr_ref, sr_ref):
    pos = pos_ref[...].astype(F32)
    ang_d = pos * invd_ref[...]
    cd_ref[...] = jnp.cos(ang_d)
    sd_ref[...] = jnp.sin(ang_d) * sgnd_ref[...]
    ang_r = pos * invr_ref[...]
    cr_ref[...] = jnp.cos(ang_r)
    sr_ref[...] = jnp.sin(ang_r) * sgnr_ref[...]


def _rotary_consts():
    lane = jnp.arange(LANES)
    half_d = ROPE_DIM // 2
    inv_d = jnp.power(jnp.float32(ROPE_THETA), -jnp.arange(half_d, dtype=F32) * (2.0 / ROPE_DIM))
    cd = lane % DIFF_SUB
    inv_d_lane = jnp.where(cd < ROPE_DIM, inv_d[cd % half_d], 0.0)
    sgn_d_lane = jnp.where(cd < half_d, -1.0, jnp.where(cd < ROPE_DIM, 1.0, 0.0))
    half_r = HEAD_DIM // 2
    inv_r = jnp.power(jnp.float32(RET_ROT_BASE), -jnp.arange(half_r, dtype=F32) * (2.0 / HEAD_DIM))
    cr = lane % HEAD_DIM
    inv_r_lane = inv_r[cr % half_r]
    sgn_r_lane = jnp.where(cr < half_r, -1.0, 1.0)
    row = lambda v: v.astype(F32).reshape(1, LANES)
    return row(inv_d_lane), row(sgn_d_lane), row(inv_r_lane), row(sgn_r_lane)


def _tables_call(positions):
    consts = _rotary_consts()
    vec = pl.BlockSpec((1, LANES), lambda i: (0, 0))
    tab = pl.BlockSpec((TAB_BLK, LANES), lambda i: (i, 0))
    return pl.pallas_call(
        _tables_kernel,
        grid=(N_TOK // TAB_BLK,),
        in_specs=[pl.BlockSpec((TAB_BLK, 1), lambda i: (i, 0)), vec, vec, vec, vec],
        out_specs=[tab, tab, tab, tab],
        out_shape=[jax.ShapeDtypeStruct((N_TOK, LANES), F32)] * 4,
        compiler_params=_params(1),
        name="rotary_tables",
    )(positions.reshape(N_TOK, 1), *consts)


def _ffn_kernel(*refs, mod_row, with_mix, with_final):
    refs = list(refs)
    x_ref, mod_ref, g_ref = refs[:3]
    del refs[:3]
    if with_mix:
        yr_ref, yd_ref, yt_ref, wm_ref = refs[:4]
        del refs[:4]
    wi_ref, wo_ref = refs[:2]
    del refs[:2]
    if with_final:
        gf_ref = refs.pop(0)
    o_ref, acc_ref = refs

    x = x_ref[...]
    if with_mix:
        y = jnp.dot(yr_ref[...], wm_ref[0:REC_WIDTH, :], preferred_element_type=F32)
        y = y + jnp.dot(yd_ref[...], wm_ref[REC_WIDTH:REC_WIDTH + DIFF_WIDTH, :], preferred_element_type=F32)
        y = y + jnp.dot(yt_ref[...], wm_ref[REC_WIDTH + DIFF_WIDTH:, :], preferred_element_type=F32)
        x = x + mod_ref[0, mod_row - 1:mod_row, :] * y
    shift = mod_ref[0, mod_row:mod_row + 1, :]
    scale = mod_ref[0, mod_row + 1:mod_row + 2, :]
    gate = mod_ref[0, mod_row + 2:mod_row + 3, :]
    h = _norm_mod(x, g_ref[...], shift, scale).astype(BF16)
    for c in range(N_FF_CHUNKS):
        lo = c * FF_CHUNK
        hg = jnp.dot(h, wi_ref[:, lo:lo + FF_CHUNK], preferred_element_type=F32)
        hu = jnp.dot(h, wi_ref[:, D_FF + lo:D_FF + lo + FF_CHUNK], preferred_element_type=F32)
        a = (hg * _sigmoid(hg) * hu).astype(BF16)
        part = jnp.dot(a, wo_ref[lo:lo + FF_CHUNK, :], preferred_element_type=F32)
        if c == 0:
            acc_ref[...] = part
        else:
            acc_ref[...] += part
    out = x + (0.5 * gate) * acc_ref[...]
    if with_final:
        ms = jnp.mean(out * out, axis=-1, keepdims=True)
        out = out * lax.rsqrt(ms + EPS) * gf_ref[...]
    o_ref[...] = out


def _ffn_call(x, mod, gains, w_in, w_out, layer, mod_row, mix=None, final_g=None):
    tok = lambda w_: pl.BlockSpec((TM_FFN, w_), lambda i: (i, 0))
    operands = [x, mod, gains]
    in_specs = [tok(D_MODEL), _mod_spec(layer, TM_FFN), _gain_spec(layer)]
    if mix is not None:
        operands += list(mix)
        in_specs += [tok(REC_WIDTH), tok(DIFF_WIDTH), tok(RET_WIDTH), _resident_layer(mix[3].shape, layer)]
    operands += [w_in, w_out]
    in_specs += [_resident_layer(w_in.shape, layer), _resident_layer(w_out.shape, layer)]
    if final_g is not None:
        operands.append(final_g.reshape(1, D_MODEL))
        in_specs.append(pl.BlockSpec((1, D_MODEL), lambda i: (0, 0)))
    return pl.pallas_call(
        functools.partial(_ffn_kernel, mod_row=mod_row, with_mix=mix is not None, with_final=final_g is not None),
        grid=(N_TOK // TM_FFN,),
        in_specs=in_specs,
        out_specs=tok(D_MODEL),
        out_shape=jax.ShapeDtypeStruct((N_TOK, D_MODEL), F32),
        scratch_shapes=[pltpu.VMEM((TM_FFN, D_MODEL), F32)],
        compiler_params=_params(1),
        name="ffn_swiglu",
    )(*operands)


def _rope(x, cos, sin, lo_mask, shift_up, shift_dn):
    outs = []
    for j in range(x.shape[1] // LANES):
        xj = x[:, j * LANES:(j + 1) * LANES]
        partner = jnp.where(lo_mask, pltpu.roll(xj, shift_up, 1), pltpu.roll(xj, shift_dn, 1))
        outs.append(xj * cos + partner * sin)
    return jnp.concatenate(outs, axis=1)


def _proj_kernel(x_ref, mod_ref, g_ref, w_ref, cd_ref, sd_ref, cr_ref, sr_ref,
                 rx_ref, rg_ref, qt_ref, k_ref, vt_ref, tq_ref, tk_ref, tv_ref, tg_ref):
    x = x_ref[...]
    shift = mod_ref[0, 3:4, :]
    scale = mod_ref[0, 4:5, :]
    h = _norm_mod(x, g_ref[...], shift, scale).astype(BF16)

    groups = {}
    for lo, hi in ((0, 768), (768, 1536), (1536, IN_WIDTH)):
        groups[lo] = jnp.dot(h, w_ref[:, lo:hi], preferred_element_type=F32)

    def proj(lo, width):
        base = max(b for b in groups if b <= lo)
        return groups[base][:, lo - base:lo - base + width]

    lane = lax.broadcasted_iota(jnp.int32, (TM, LANES), 1)
    lo_d = (lane & (DIFF_SUB - 1)) < (ROPE_DIM // 2)
    lo_r = (lane & (HEAD_DIM - 1)) < (HEAD_DIM // 2)
    cd, sd, cr, sr = cd_ref[...], sd_ref[...], cr_ref[...], sr_ref[...]
    rope_d = lambda v: _rope(v, cd, sd, lo_d, LANES - ROPE_DIM // 2, ROPE_DIM // 2)
    rope_r = lambda v: _rope(v, cr, sr, lo_r, LANES - HEAD_DIM // 2, HEAD_DIM // 2)

    rx_ref[...] = proj(0, REC_WIDTH)
    rg_ref[...] = proj(384, REC_WIDTH)
    q = rope_d(proj(768, DIFF_WIDTH)) * (DIFF_SUB ** -0.5 * math.log2(math.e))
    qt = q.T
    k_ref[...] = rope_d(proj(1152, DIFF_WIDTH)).astype(BF16)
    vt = proj(1536, DIFF_WIDTH).T
    for j in range(TM // ATT_BLK):
        qt_ref[0, j] = qt[:, j * ATT_BLK:(j + 1) * ATT_BLK].astype(BF16)
        vt_ref[0, j] = vt[:, j * ATT_BLK:(j + 1) * ATT_BLK].astype(BF16)
    tq_ref[...] = rope_r(proj(1920, RET_WIDTH)).astype(BF16)
    tk_ref[...] = (rope_r(proj(2176, RET_WIDTH)) * (HEAD_DIM ** -0.5)).astype(BF16)
    tv_ref[...] = proj(2432, RET_WIDTH).astype(BF16)
    tg_ref[...] = proj(2688, RET_WIDTH)


def _proj_call(x, mod, gains, w, tables, layer):
    tiles_per_batch = SEQ // TM
    blocks_per_tile = TM // ATT_BLK
    n_blk = SEQ // ATT_BLK
    tok = lambda w_, : pl.BlockSpec((TM, w_), lambda i: (i, 0))
    tspec = pl.BlockSpec((1, blocks_per_tile, DIFF_WIDTH, ATT_BLK),
                         lambda i: (i // tiles_per_batch, i % tiles_per_batch, 0, 0))
    tshape = jax.ShapeDtypeStruct((BATCH, n_blk, DIFF_WIDTH, ATT_BLK), BF16)
    tok_shape = lambda w_, dt: jax.ShapeDtypeStruct((N_TOK, w_), dt)
    return pl.pallas_call(
        _proj_kernel,
        grid=(N_TOK // TM,),
        in_specs=[
            tok(D_MODEL), _mod_spec(layer, TM), _gain_spec(layer), _resident_layer(w.shape, layer),
            tok(LANES), tok(LANES), tok(LANES), tok(LANES),
        ],
        out_specs=[tok(REC_WIDTH), tok(REC_WIDTH), tspec, tok(DIFF_WIDTH), tspec,
                   tok(RET_WIDTH), tok(RET_WIDTH), tok(RET_WIDTH), tok(RET_WIDTH)],
        out_shape=[tok_shape(REC_WIDTH, F32), tok_shape(REC_WIDTH, F32), tshape, tok_shape(DIFF_WIDTH, BF16), tshape,
                   tok_shape(RET_WIDTH, BF16), tok_shape(RET_WIDTH, BF16), tok_shape(RET_WIDTH, BF16),
                   tok_shape(RET_WIDTH, F32)],
        compiler_params=_params(1),
        name="mixer_in_proj",
    )(x, mod, gains, w, *tables)


def _rglru_kernel(rx_ref, rg_ref, cw_ref, cb_ref, wa_ref, ba_ref, wx_ref, bx_ref, lam_ref,
                  y_ref, xe_ref, h_ref):
    @pl.when(pl.program_id(1) == 0)
    def _():
        xe_ref[...] = jnp.zeros_like(xe_ref)
        h_ref[...] = jnp.zeros_like(h_ref)

    n_grp = LRU_BLK // SUBLANES
    x = rx_ref[...]
    x3 = x.reshape(n_grp, SUBLANES, REC_WIDTH)
    tail = xe_ref[...]
    within = lax.broadcasted_iota(jnp.int32, (n_grp, SUBLANES, REC_WIDTH), 1)
    cw = cw_ref[...]
    xc3 = cb_ref[...] + cw[REC_CONV - 1:REC_CONV, :] * x3
    for k in range(1, REC_CONV):
        rot = pltpu.roll(x3, k, 1)
        before = jnp.concatenate([pltpu.roll(tail, k, 0)[None], rot[:-1]], axis=0)
        xc3 = xc3 + cw[REC_CONV - 1 - k:REC_CONV - k, :] * jnp.where(within >= k, rot, before)
    xe_ref[...] = x[LRU_BLK - SUBLANES:, :]
    xc = xc3.reshape(LRU_BLK, REC_WIDTH)

    xb = xc.astype(BF16)
    r = _sigmoid(jnp.dot(xb, wa_ref[...], preferred_element_type=F32) + ba_ref[...])
    gi = _sigmoid(jnp.dot(xb, wx_ref[...], preferred_element_type=F32) + bx_ref[...])
    nl = -lam_ref[...]
    softplus = jnp.maximum(nl, 0.0) + jnp.log1p(jnp.exp(-jnp.abs(nl)))
    log_a = -REC_C * r * softplus
    a = jnp.exp(log_a)
    b = jnp.sqrt(-jnp.tanh(log_a) * (a * a + 1.0)) * (gi * xc)

    a = a.reshape(n_grp, SUBLANES, REC_WIDTH)
    b = b.reshape(n_grp, SUBLANES, REC_WIDTH)
    s = 1
    while s < SUBLANES:
        valid = within >= s
        a_prev = pltpu.roll(a, s, 1)
        b_prev = pltpu.roll(b, s, 1)
        b = jnp.where(valid, a * b_prev, 0.0) + b
        a = jnp.where(valid, a * a_prev, a)
        s *= 2
    carry = h_ref[...]
    groups = []
    for grp in range(n_grp):
        hg = a[grp] * carry + b[grp]
        groups.append(hg)
        carry = hg[SUBLANES - 1:, :]
    h = jnp.concatenate(groups, axis=0)
    h_ref[...] = carry

    gate = rg_ref[...]
    cdf = 0.5 * (1.0 + jnp.tanh(math.sqrt(2.0 / math.pi) * (gate + 0.044715 * (gate * gate * gate))))
    y_ref[...] = (gate * cdf * h).astype(BF16)


def _rglru_call(rx, rg, conv_w, conv_b, wa_bd, ba, wx_bd, bx, lam):
    nt = SEQ // LRU_BLK
    tok = pl.BlockSpec((LRU_BLK, REC_WIDTH), lambda b, t: (b * nt + t, 0))
    vec = pl.BlockSpec((1, REC_WIDTH), lambda b, t: (0, 0))
    mat = pl.BlockSpec((REC_WIDTH, REC_WIDTH), lambda b, t: (0, 0))
    row = lambda v: v.reshape(1, REC_WIDTH)
    return pl.pallas_call(
        _rglru_kernel,
        grid=(BATCH, nt),
        in_specs=[tok, tok, pl.BlockSpec((REC_CONV, REC_WIDTH), lambda b, t: (0, 0)), vec, mat, vec, mat, vec, vec],
        out_specs=tok,
        out_shape=jax.ShapeDtypeStruct((N_TOK, REC_WIDTH), BF16),
        scratch_shapes=[pltpu.VMEM((SUBLANES, REC_WIDTH), F32), pltpu.VMEM((1, REC_WIDTH), F32)],
        compiler_params=_params(2),
        name="rglru",
    )(rx, rg, conv_w, row(conv_b), wa_bd, row(ba), wx_bd, row(bx), row(lam))


def _block_diag(w):
    h, d, _ = w.shape
    eye = jnp.eye(h, dtype=w.dtype)
    return jnp.einsum('hij,hk->hikj', w, eye).reshape(h * d, h * d)


def _diff_kernel(lam_ref, qt_ref, k_ref, vt_ref, g_ref, o_ref,
                 qm_ref, st_ref, mb_ref, p_ref, al_ref, m_ref, acc_ref, *, lambda_init):
    i = pl.program_id(2)
    n_grp = qt_ref.shape[2] // LANES
    n_sc = 4 * n_grp
    rowq = lax.broadcasted_iota(jnp.int32, (LANES, ATT_BLK), 0)
    for idx in range(n_sc):
        grp, sub = divmod(idx, 4)
        qt = qt_ref[0, 0, grp * LANES:(grp + 1) * LANES, :]
        band = (rowq >= sub * DIFF_SUB) & (rowq < (sub + 1) * DIFF_SUB)
        qm_ref[idx] = jnp.where(band, qt, jnp.zeros_like(qt))
        m_ref[idx] = jnp.full((1, ATT_BLK), NEG_BIG, F32)
        acc_ref[idx] = jnp.zeros((ATT_ACC_ROWS, ATT_BLK), F32)
    st_ref[1] = jnp.full(st_ref.shape[1:], 2.0 * NEG_BIG, F32)
    mb_ref[1] = jnp.full(mb_ref.shape[1:], 2.0 * NEG_BIG, F32)
    p_ref[0] = jnp.zeros(p_ref.shape[1:], BF16)
    al_ref[0] = jnp.ones(al_ref.shape[1:], F32)
    ones_rows = jnp.ones((ATT_ACC_ROWS - HEAD_DIM, ATT_BLK), BF16)

    def scores(t, slot, masked):
        rows = pl.ds(pl.multiple_of(t * ATT_BLK, ATT_BLK), ATT_BLK)
        if masked:
            kk = lax.broadcasted_iota(jnp.int32, (ATT_BLK, ATT_BLK), 0)
            qq = lax.broadcasted_iota(jnp.int32, (ATT_BLK, ATT_BLK), 1)
            causal = kk <= qq
        for grp in range(n_grp):
            kb = k_ref[0, rows, grp * LANES:(grp + 1) * LANES]
            for idx in range(4 * grp, 4 * grp + 4):
                st = jnp.dot(kb, qm_ref[idx], preferred_element_type=F32)
                if masked:
                    st = jnp.where(causal, st, NEG_BIG)
                st_ref[slot, idx] = st
                mb_ref[slot, idx] = jnp.max(st, axis=0, keepdims=True)

    def softmax(slot):
        for idx in range(n_sc):
            m_old = m_ref[idx]
            m_new = jnp.maximum(m_old, mb_ref[slot, idx])
            p_ref[slot, idx] = jnp.exp2(st_ref[slot, idx] - m_new).astype(BF16)
            al_ref[slot, idx] = jnp.exp2(m_old - m_new)
            m_ref[idx] = m_new

    def values(t, slot):
        blk = jnp.maximum(t, 0)
        for head in range(2 * n_grp):
            vh = jnp.concatenate([vt_ref[0, blk, head * HEAD_DIM:(head + 1) * HEAD_DIM, :], ones_rows], axis=0)
            for idx in (2 * head, 2 * head + 1):
                acc_ref[idx] = (al_ref[slot, idx] * acc_ref[idx]
                                + jnp.dot(vh, p_ref[slot, idx], preferred_element_type=F32))

    def advance(t, slot, masked):
        values(t - 2, slot)
        softmax(1 - slot)
        scores(t, slot, masked)

    def drain(last_slot):
        values(i - 1, 1 - last_slot)
        softmax(last_slot)
        values(i, last_slot)

    def group(u, carry):
        for j in range(ATT_UNROLL):
            advance(ATT_UNROLL * u + j, j % 2, False)
        return carry

    lax.fori_loop(0, i // ATT_UNROLL, group, 0)

    for rem in range(ATT_UNROLL):
        @pl.when(i % ATT_UNROLL == rem)
        def _(rem=rem):
            for j in range(rem):
                advance(i - rem + j, j % 2, False)
            advance(i, rem % 2, True)
            drain(rem % 2)

    lv = lam_ref[...]
    lam = (jnp.exp(jnp.sum(lv[0:1] * lv[1:2], axis=-1, keepdims=True))
           - jnp.exp(jnp.sum(lv[2:3] * lv[3:4], axis=-1, keepdims=True)) + lambda_init)
    for grp in range(n_grp):
        outs = []
        for head in (2 * grp, 2 * grp + 1):
            a0, a1 = acc_ref[2 * head], acc_ref[2 * head + 1]
            o = (a0[:HEAD_DIM] / a0[HEAD_DIM:HEAD_DIM + 1]
                 - lam * (a1[:HEAD_DIM] / a1[HEAD_DIM:HEAD_DIM + 1]))
            ms = jnp.mean(o * o, axis=0, keepdims=True)
            outs.append(o * lax.rsqrt(ms + EPS) * g_ref[...] * (1.0 - lambda_init))
        o_ref[:, grp * LANES:(grp + 1) * LANES] = jnp.concatenate(outs, axis=0).T.astype(BF16)


def _diff_call(lam_vecs, qt, k, vt, subln_g, lambda_init):
    nq = SEQ // ATT_BLK
    width = ATT_GROUPS * LANES
    n_sc = 4 * ATT_GROUPS
    return pl.pallas_call(
        functools.partial(_diff_kernel, lambda_init=lambda_init),
        grid=(BATCH, DIFF_WIDTH // width, nq),
        in_specs=[
            pl.BlockSpec((4, LANES), lambda b, g, i: (0, 0)),
            pl.BlockSpec((1, 1, width, ATT_BLK), lambda b, g, i: (b, i, g, 0)),
            pl.BlockSpec((1, SEQ, width), lambda b, g, i: (b, 0, g)),
            pl.BlockSpec((1, nq, width, ATT_BLK), lambda b, g, i: (b, 0, g, 0)),
            pl.BlockSpec((HEAD_DIM, 1), lambda b, g, i: (0, 0)),
        ],
        out_specs=pl.BlockSpec((ATT_BLK, width), lambda b, g, i: (b * nq + i, g)),
        out_shape=jax.ShapeDtypeStruct((N_TOK, DIFF_WIDTH), BF16),
        scratch_shapes=[
            pltpu.VMEM((n_sc, LANES, ATT_BLK), BF16),
            pltpu.VMEM((2, n_sc, ATT_BLK, ATT_BLK), F32),
            pltpu.VMEM((2, n_sc, 1, ATT_BLK), F32),
            pltpu.VMEM((2, n_sc, ATT_BLK, ATT_BLK), BF16),
            pltpu.VMEM((2, n_sc, 1, ATT_BLK), F32),
            pltpu.VMEM((n_sc, 1, ATT_BLK), F32),
            pltpu.VMEM((n_sc, ATT_ACC_ROWS, ATT_BLK), F32),
        ],
        compiler_params=_params(3),
        name="diff_attn",
    )(lam_vecs, qt, k.reshape(BATCH, SEQ, DIFF_WIDTH), vt, subln_g.reshape(HEAD_DIM, 1))


def _ret_kernel(q_ref, k_ref, v_ref, g_ref, dec_ref, kdec_ref, qdec_ref, cdec_ref, bmask_ref, o_ref, s_ref):
    @pl.when(pl.program_id(1) == 0)
    def _():
        s_ref[...] = jnp.zeros_like(s_ref)

    lane = lax.broadcasted_iota(jnp.int32, (RET_BLK, LANES), 1)
    first = lane < HEAD_DIM
    for pair in range(RET_HEADS // 2):
        lanes = slice(pair * LANES, (pair + 1) * LANES)
        state = s_ref[pair]
        for chunk in range(RET_TILE // RET_BLK):
            rows = slice(chunk * RET_BLK, (chunk + 1) * RET_BLK)
            q = q_ref[rows, lanes]
            k = k_ref[rows, lanes]
            v = v_ref[rows, lanes]
            o = jnp.dot(q, state.astype(BF16), preferred_element_type=F32) * qdec_ref[pair]
            for hh in range(2):
                sel = first if hh == 0 else jnp.logical_not(first)
                qm = jnp.where(sel, q, jnp.zeros_like(q))
                sc = lax.dot_general(qm, k, (((1,), (1,)), ((), ())),
                                     preferred_element_type=F32) * dec_ref[pair, hh]
                oi = jnp.dot(sc.astype(BF16), v, preferred_element_type=F32)
                o = o + jnp.where(sel, oi, 0.0)
            kd_t = (k.astype(F32) * kdec_ref[pair]).T.astype(BF16)
            kv = jnp.dot(kd_t, v, preferred_element_type=F32)
            state = state * cdec_ref[pair] + kv * bmask_ref[...]

            o2 = o * o
            ms0 = jnp.sum(jnp.where(first, o2, 0.0), axis=-1, keepdims=True) * (1.0 / HEAD_DIM)
            ms1 = jnp.sum(jnp.where(first, 0.0, o2), axis=-1, keepdims=True) * (1.0 / HEAD_DIM)
            rs = jnp.where(first, lax.rsqrt(ms0 + EPS), lax.rsqrt(ms1 + EPS))
            gate = g_ref[rows, lanes]
            o_ref[rows, lanes] = (gate * _sigmoid(gate) * (o * rs)).astype(BF16)
        s_ref[pair] = state


def _ret_consts():
    c = RET_BLK
    lg = jnp.log1p(-jnp.exp2(-5.0 - jnp.arange(RET_HEADS, dtype=F32)))
    idx = jnp.arange(c, dtype=F32)
    rel = idx[:, None] - idx[None, :]
    decay = jnp.where(rel >= 0, jnp.exp(jnp.maximum(rel, 0.0)[None] * lg[:, None, None]), 0.0)
    k_decay = jnp.exp((c - 1.0 - idx)[None, :] * lg[:, None])
    q_decay = jnp.exp((idx + 1.0)[None, :] * lg[:, None])
    chunk_decay = jnp.exp(c * lg)
    pairs = RET_HEADS // 2
    lanes = lambda t: jnp.repeat(t.reshape(pairs, 2, c).transpose(0, 2, 1), HEAD_DIM, axis=2)
    cdec = jnp.broadcast_to(jnp.repeat(chunk_decay.reshape(pairs, 2), HEAD_DIM, axis=1)[:, :, None],
                            (pairs, LANES, LANES))
    blk = jnp.arange(LANES) // HEAD_DIM
    bmask = (blk[:, None] == blk[None, :]).astype(F32)
    return decay.reshape(pairs, 2, c, c), lanes(k_decay), lanes(q_decay), cdec, bmask


def _ret_call(tq, tk, tv, tg, consts):
    decay, kdec, qdec, cdec, bmask = consts
    nt = SEQ // RET_TILE
    tok = pl.BlockSpec((RET_TILE, RET_WIDTH), lambda b, t: (b * nt + t, 0))
    whole = lambda a: pl.BlockSpec(a.shape, lambda b, t: (0,) * a.ndim)
    return pl.pallas_call(
        _ret_kernel,
        grid=(BATCH, nt),
        in_specs=[tok, tok, tok, tok, whole(decay), whole(kdec), whole(qdec), whole(cdec), whole(bmask)],
        out_specs=tok,
        out_shape=jax.ShapeDtypeStruct((N_TOK, RET_WIDTH), BF16),
        scratch_shapes=[pltpu.VMEM((RET_HEADS // 2, LANES, LANES), F32)],
        compiler_params=_params(2),
        name="retention",
    )(tq, tk, tv, tg, decay, kdec, qdec, cdec, bmask)


def kernel(x, c, positions, norm_ffn1_g, norm_mix_g, norm_ffn2_g, ada_w, ada_b, ffn1_w_in, ffn1_w_out, ffn2_w_in, ffn2_w_out, w_in, w_out, rec_conv_w, rec_conv_b, rec_gate_a_w, rec_gate_a_b, rec_gate_x_w, rec_gate_x_b, rec_lambda, diff_lambda_q1, diff_lambda_k1, diff_lambda_q2, diff_lambda_k2, diff_subln_g, final_norm_g):
    xf = x.reshape(N_TOK, D_MODEL)
    mod = _ada_call(c, ada_w, ada_b).reshape(DEPTH, BATCH, N_MOD, D_MODEL)
    tables = _tables_call(positions)
    ret_consts = _ret_consts()
    bf = lambda w: w.astype(BF16)
    ffn1_wi, ffn1_wo, ffn2_wi, ffn2_wo, mix_wi, mix_wo = map(bf, (ffn1_w_in, ffn1_w_out, ffn2_w_in, ffn2_w_out,
                                                                  w_in, w_out))
    gains = lambda g: g[:, None, :]
    g_ffn1, g_mix, g_ffn2 = gains(norm_ffn1_g), gains(norm_mix_g), gains(norm_ffn2_g)
    for l in range(DEPTH):
        xf = _ffn_call(xf, mod, g_ffn1, ffn1_wi, ffn1_wo, l, mod_row=0)

        rx, rg, qt, k, vt, tq, tk, tv, tg = _proj_call(xf, mod, g_mix, mix_wi, tables, l)
        y_rec = _rglru_call(rx, rg, rec_conv_w[l], rec_conv_b[l],
                            _block_diag(rec_gate_a_w[l]).astype(BF16), rec_gate_a_b[l],
                            _block_diag(rec_gate_x_w[l]).astype(BF16), rec_gate_x_b[l], rec_lambda[l])
        lambda_init = 0.8 - 0.6 * math.exp(-0.3 * l)
        lam_vecs = jnp.stack([diff_lambda_q1[l], diff_lambda_k1[l], diff_lambda_q2[l], diff_lambda_k2[l]])
        lam_vecs = jnp.pad(lam_vecs, ((0, 0), (0, LANES - DIFF_SUB)))
        y_diff = _diff_call(lam_vecs, qt, k, vt, diff_subln_g[l], lambda_init)
        y_ret = _ret_call(tq, tk, tv, tg, ret_consts)

        xf = _ffn_call(xf, mod, g_ffn2, ffn2_wi, ffn2_wo, l, mod_row=6, mix=(y_rec, y_diff, y_ret, mix_wo),
                       final_g=final_norm_g if l == DEPTH - 1 else None)
    return xf.reshape(BATCH, SEQ, D_MODEL)
```

```python
import functools
import math

import jax
import jax.numpy as jnp
from jax import lax
from jax.experimental import pallas as pl
from jax.experimental.pallas import tpu as pltpu

F32 = jnp.float32
BF16 = jnp.bfloat16

D_MODEL = 1024
BATCH = 4
SEQ = 4096
DEPTH = 4
N_TOK = BATCH * SEQ
HEAD_DIM = 64
REC_WIDTH = 384
REC_HEADS = 6
REC_CONV = 4
REC_C = 8.0
DIFF_WIDTH = 384
DIFF_HEADS = 6
DIFF_SUB = 32
ROPE_THETA = 500000.0
ROPE_DIM = 8
RET_WIDTH = 256
RET_HEADS = 4
RET_ROT_BASE = 10000.0
IN_WIDTH = 2944
D_FF = 2816
N_MOD = 9
EPS = 1e-6

LANES = 128
SUBLANES = 8
MXU_DIM = 256
VMEM_LIMIT_BYTES = 56 * 1024 * 1024

TM = 512
TM_FFN = 1024
FF_CHUNK = MXU_DIM
N_FF_CHUNKS = D_FF // FF_CHUNK
ATT_BLK = 256
ATT_ACC_ROWS = HEAD_DIM + 16
ATT_UNROLL = 4
ATT_GROUPS = 1
RET_BLK = 256
RET_TILE = 1024
LRU_BLK = 512
TAB_BLK = 1024
NEG_BIG = -1e30


def _sigmoid(x):
    return 1.0 / (1.0 + jnp.exp(-x))


def _norm_mod(x, g, shift, scale):
    ms = jnp.mean(x * x, axis=-1, keepdims=True)
    y = x * lax.rsqrt(ms + EPS) * g
    return y * (1.0 + scale) + shift


def _resident_layer(stacked_shape, layer):
    nd = len(stacked_shape) - 1
    return pl.BlockSpec((None,) + tuple(stacked_shape[1:]), lambda *_: (layer,) + (0,) * nd,
                        pipeline_mode=pl.Buffered(1))


def _mod_spec(layer, rows_per_step):
    steps_per_batch = SEQ // rows_per_step
    return pl.BlockSpec((None, 1, N_MOD, D_MODEL), lambda i: (layer, i // steps_per_batch, 0, 0))


def _gain_spec(layer):
    return pl.BlockSpec((None, 1, D_MODEL), lambda i: (layer, 0, 0))


def _params(n_axes, vmem=VMEM_LIMIT_BYTES, flags=None):
    return pltpu.CompilerParams(dimension_semantics=("arbitrary",) * n_axes, vmem_limit_bytes=vmem, flags=flags)


def _ada_kernel(c_ref, w_ref, b_ref, o_ref):
    c = c_ref[...]
    s = c * _sigmoid(c)
    o_ref[0] = jnp.dot(s, w_ref[0], preferred_element_type=F32, precision=lax.Precision.HIGHEST) + b_ref[0]


def _ada_call(c, ada_w, ada_b):
    tn = 1536
    nmod = N_MOD * D_MODEL
    return pl.pallas_call(
        _ada_kernel,
        grid=(DEPTH, nmod // tn),
        in_specs=[
            pl.BlockSpec((BATCH, D_MODEL), lambda l, j: (0, 0)),
            pl.BlockSpec((1, D_MODEL, tn), lambda l, j: (l, 0, j)),
            pl.BlockSpec((1, 1, tn), lambda l, j: (l, 0, j)),
        ],
        out_specs=pl.BlockSpec((1, BATCH, tn), lambda l, j: (l, 0, j)),
        out_shape=jax.ShapeDtypeStruct((DEPTH, BATCH, nmod), F32),
        compiler_params=_params(2),
        name="ada_mod",
    )(c, ada_w, ada_b.reshape(DEPTH, 1, nmod))


def _tables_kernel(pos_ref, invd_ref, sgnd_ref, invr_ref, sgnr_ref, cd_ref, sd_ref, cr_ref, sr_ref):
    pos = pos_ref[...].astype(F32)
    ang_d = pos * invd_ref[...]
    cd_ref[...] = jnp.cos(ang_d)
    sd_ref[...] = jnp.sin(ang_d) * sgnd_ref[...]
    ang_r = pos * invr_ref[...]
    cr_ref[...] = jnp.cos(ang_r)
    sr_ref[...] = jnp.sin(ang_r) * sgnr_ref[...]


def _rotary_consts():
    lane = jnp.arange(LANES)
    half_d = ROPE_DIM // 2
    inv_d = jnp.power(jnp.float32(ROPE_THETA), -jnp.arange(half_d, dtype=F32) * (2.0 / ROPE_DIM))
    cd = lane % DIFF_SUB
    inv_d_lane = jnp.where(cd < ROPE_DIM, inv_d[cd % half_d], 0.0)
    sgn_d_lane = jnp.where(cd < half_d, -1.0, jnp.where(cd < ROPE_DIM, 1.0, 0.0))
    half_r = HEAD_DIM // 2
    inv_r = jnp.power(jnp.float32(RET_ROT_BASE), -jnp.arange(half_r, dtype=F32) * (2.0 / HEAD_DIM))
    cr = lane % HEAD_DIM
    inv_r_lane = inv_r[cr % half_r]
    sgn_r_lane = jnp.where(cr < half_r, -1.0, 1.0)
    row = lambda v: v.astype(F32).reshape(1, LANES)
    return row(inv_d_lane), row(sgn_d_lane), row(inv_r_lane), row(sgn_r_lane)


def _tables_call(positions):
    consts = _rotary_consts()
    vec = pl.BlockSpec((1, LANES), lambda i: (0, 0))
    tab = pl.BlockSpec((TAB_BLK, LANES), lambda i: (i, 0))
    return pl.pallas_call(
        _tables_kernel,
        grid=(N_TOK // TAB_BLK,),
        in_specs=[pl.BlockSpec((TAB_BLK, 1), lambda i: (i, 0)), vec, vec, vec, vec],
        out_specs=[tab, tab, tab, tab],
        out_shape=[jax.ShapeDtypeStruct((N_TOK, LANES), F32)] * 4,
        compiler_params=_params(1),
        name="rotary_tables",
    )(positions.reshape(N_TOK, 1), *consts)


def _ffn_kernel(*refs, mod_row, with_mix, with_final):
    refs = list(refs)
    x_ref, mod_ref, g_ref = refs[:3]
    del refs[:3]
    if with_mix:
        yr_ref, yd_ref, yt_ref, wm_ref = refs[:4]
        del refs[:4]
    wi_ref, wo_ref = refs[:2]
    del refs[:2]
    if with_final:
        gf_ref = refs.pop(0)
    o_ref, acc_ref = refs

    x = x_ref[...]
    if with_mix:
        y_cat = jnp.concatenate([yr_ref[...], yd_ref[...], yt_ref[...]], axis=1)
        y = jnp.dot(y_cat, wm_ref[...], preferred_element_type=F32)
        x = x + mod_ref[0, mod_row - 1:mod_row, :] * y
    shift = mod_ref[0, mod_row:mod_row + 1, :]
    scale = mod_ref[0, mod_row + 1:mod_row + 2, :]
    gate = mod_ref[0, mod_row + 2:mod_row + 3, :]
    h = _norm_mod(x, g_ref[...], shift, scale).astype(BF16)
    for c in range(N_FF_CHUNKS):
        lo = c * FF_CHUNK
        hg = jnp.dot(h, wi_ref[:, lo:lo + FF_CHUNK], preferred_element_type=F32)
        hu = jnp.dot(h, wi_ref[:, D_FF + lo:D_FF + lo + FF_CHUNK], preferred_element_type=F32)
        a = (hg * _sigmoid(hg) * hu).astype(BF16)
        part = jnp.dot(a, wo_ref[lo:lo + FF_CHUNK, :], preferred_element_type=F32)
        if c == 0:
            acc_ref[...] = part
        else:
            acc_ref[...] += part
    out = x + (0.5 * gate) * acc_ref[...]
    if with_final:
        ms = jnp.mean(out * out, axis=-1, keepdims=True)
        out = out * lax.rsqrt(ms + EPS) * gf_ref[...]
    o_ref[...] = out


def _ffn_call(x, mod, gains, w_in, w_out, layer, mod_row, mix=None, final_g=None):
    tok = lambda w_: pl.BlockSpec((TM_FFN, w_), lambda i: (i, 0))
    operands = [x, mod, gains]
    in_specs = [tok(D_MODEL), _mod_spec(layer, TM_FFN), _gain_spec(layer)]
    if mix is not None:
        operands += list(mix)
        in_specs += [tok(REC_WIDTH), tok(DIFF_WIDTH), tok(RET_WIDTH), _resident_layer(mix[3].shape, layer)]
    operands += [w_in, w_out]
    in_specs += [_resident_layer(w_in.shape, layer), _resident_layer(w_out.shape, layer)]
    if final_g is not None:
        operands.append(final_g.reshape(1, D_MODEL))
        in_specs.append(pl.BlockSpec((1, D_MODEL), lambda i: (0, 0)))
    return pl.pallas_call(
        functools.partial(_ffn_kernel, mod_row=mod_row, with_mix=mix is not None, with_final=final_g is not None),
        grid=(N_TOK // TM_FFN,),
        in_specs=in_specs,
        out_specs=tok(D_MODEL),
        out_shape=jax.ShapeDtypeStruct((N_TOK, D_MODEL), F32),
        scratch_shapes=[pltpu.VMEM((TM_FFN, D_MODEL), F32)],
        compiler_params=_params(1),
        name="ffn_swiglu",
    )(*operands)


def _rope(x, cos, sin, lo_mask, shift_up, shift_dn):
    outs = []
    for j in range(x.shape[1] // LANES):
        xj = x[:, j * LANES:(j + 1) * LANES]
        partner = jnp.where(lo_mask, pltpu.roll(xj, shift_up, 1), pltpu.roll(xj, shift_dn, 1))
        outs.append(xj * cos + partner * sin)
    return jnp.concatenate(outs, axis=1)


def _proj_kernel(x_ref, mod_ref, g_ref, w_ref, cd_ref, sd_ref, cr_ref, sr_ref,
                 rx_ref, rg_ref, qt_ref, k_ref, vt_ref, tq_ref, tk_ref, tv_ref, tg_ref):
    x = x_ref[...]
    shift = mod_ref[0, 3:4, :]
    scale = mod_ref[0, 4:5, :]
    h = _norm_mod(x, g_ref[...], shift, scale).astype(BF16)

    groups = {}
    for lo, hi in ((0, 768), (768, 1536), (1536, IN_WIDTH)):
        groups[lo] = jnp.dot(h, w_ref[:, lo:hi], preferred_element_type=F32)

    def proj(lo, width):
        base = max(b for b in groups if b <= lo)
        return groups[base][:, lo - base:lo - base + width]

    lane = lax.broadcasted_iota(jnp.int32, (TM, LANES), 1)
    lo_d = (lane & (DIFF_SUB - 1)) < (ROPE_DIM // 2)
    lo_r = (lane & (HEAD_DIM - 1)) < (HEAD_DIM // 2)
    cd, sd, cr, sr = cd_ref[...], sd_ref[...], cr_ref[...], sr_ref[...]
    rope_d = lambda v: _rope(v, cd, sd, lo_d, LANES - ROPE_DIM // 2, ROPE_DIM // 2)
    rope_r = lambda v: _rope(v, cr, sr, lo_r, LANES - HEAD_DIM // 2, HEAD_DIM // 2)

    rx_ref[...] = proj(0, REC_WIDTH)
    rg_ref[...] = proj(384, REC_WIDTH)
    q = rope_d(proj(768, DIFF_WIDTH)) * (DIFF_SUB ** -0.5 * math.log2(math.e))
    qt = q.T
    k_ref[...] = rope_d(proj(1152, DIFF_WIDTH)).astype(BF16)
    vt = proj(1536, DIFF_WIDTH).T
    for j in range(TM // ATT_BLK):
        qt_ref[0, j] = qt[:, j * ATT_BLK:(j + 1) * ATT_BLK].astype(BF16)
        vt_ref[0, j] = vt[:, j * ATT_BLK:(j + 1) * ATT_BLK].astype(BF16)
    tq_ref[...] = rope_r(proj(1920, RET_WIDTH)).astype(BF16)
    tk_ref[...] = (rope_r(proj(2176, RET_WIDTH)) * (HEAD_DIM ** -0.5)).astype(BF16)
    tv_ref[...] = proj(2432, RET_WIDTH).astype(BF16)
    tg_ref[...] = proj(2688, RET_WIDTH)


def _proj_call(x, mod, gains, w, tables, layer):
    tiles_per_batch = SEQ // TM
    blocks_per_tile = TM // ATT_BLK
    n_blk = SEQ // ATT_BLK
    tok = lambda w_, : pl.BlockSpec((TM, w_), lambda i: (i, 0))
    tspec = pl.BlockSpec((1, blocks_per_tile, DIFF_WIDTH, ATT_BLK),
                         lambda i: (i // tiles_per_batch, i % tiles_per_batch, 0, 0))
    tshape = jax.ShapeDtypeStruct((BATCH, n_blk, DIFF_WIDTH, ATT_BLK), BF16)
    tok_shape = lambda w_, dt: jax.ShapeDtypeStruct((N_TOK, w_), dt)
    return pl.pallas_call(
        _proj_kernel,
        grid=(N_TOK // TM,),
        in_specs=[
            tok(D_MODEL), _mod_spec(layer, TM), _gain_spec(layer), _resident_layer(w.shape, layer),
            tok(LANES), tok(LANES), tok(LANES), tok(LANES),
        ],
        out_specs=[tok(REC_WIDTH), tok(REC_WIDTH), tspec, tok(DIFF_WIDTH), tspec,
                   tok(RET_WIDTH), tok(RET_WIDTH), tok(RET_WIDTH), tok(RET_WIDTH)],
        out_shape=[tok_shape(REC_WIDTH, F32), tok_shape(REC_WIDTH, F32), tshape, tok_shape(DIFF_WIDTH, BF16), tshape,
                   tok_shape(RET_WIDTH, BF16), tok_shape(RET_WIDTH, BF16), tok_shape(RET_WIDTH, BF16),
                   tok_shape(RET_WIDTH, F32)],
        compiler_params=_params(1),
        name="mixer_in_proj",
    )(x, mod, gains, w, *tables)


def _rglru_kernel(rx_ref, rg_ref, cw_ref, cb_ref, wa_ref, ba_ref, wx_ref, bx_ref, lam_ref,
                  y_ref, xe_ref, h_ref):
    @pl.when(pl.program_id(1) == 0)
    def _():
        xe_ref[...] = jnp.zeros_like(xe_ref)
        h_ref[...] = jnp.zeros_like(h_ref)

    n_grp = LRU_BLK // SUBLANES
    x = rx_ref[...]
    x3 = x.reshape(n_grp, SUBLANES, REC_WIDTH)
    tail = xe_ref[...]
    within = lax.broadcasted_iota(jnp.int32, (n_grp, SUBLANES, REC_WIDTH), 1)
    cw = cw_ref[...]
    xc3 = cb_ref[...] + cw[REC_CONV - 1:REC_CONV, :] * x3
    for k in range(1, REC_CONV):
        rot = pltpu.roll(x3, k, 1)
        before = jnp.concatenate([pltpu.roll(tail, k, 0)[None], rot[:-1]], axis=0)
        xc3 = xc3 + cw[REC_CONV - 1 - k:REC_CONV - k, :] * jnp.where(within >= k, rot, before)
    xe_ref[...] = x[LRU_BLK - SUBLANES:, :]
    xc = xc3.reshape(LRU_BLK, REC_WIDTH)

    xb = xc.astype(BF16)
    r = _sigmoid(jnp.dot(xb, wa_ref[...], preferred_element_type=F32) + ba_ref[...])
    gi = _sigmoid(jnp.dot(xb, wx_ref[...], preferred_element_type=F32) + bx_ref[...])
    nl = -lam_ref[...]
    softplus = jnp.maximum(nl, 0.0) + jnp.log1p(jnp.exp(-jnp.abs(nl)))
    log_a = -REC_C * r * softplus
    a = jnp.exp(log_a)
    b = jnp.sqrt(-jnp.tanh(log_a) * (a * a + 1.0)) * (gi * xc)

    a = a.reshape(n_grp, SUBLANES, REC_WIDTH)
    b = b.reshape(n_grp, SUBLANES, REC_WIDTH)
    s = 1
    while s < SUBLANES:
        valid = within >= s
        a_prev = pltpu.roll(a, s, 1)
        b_prev = pltpu.roll(b, s, 1)
        b = jnp.where(valid, a * b_prev, 0.0) + b
        a = jnp.where(valid, a * a_prev, a)
        s *= 2
    carry = h_ref[...]
    groups = []
    for grp in range(n_grp):
        hg = a[grp] * carry + b[grp]
        groups.append(hg)
        carry = hg[SUBLANES - 1:, :]
    h = jnp.concatenate(groups, axis=0)
    h_ref[...] = carry

    gate = rg_ref[...]
    cdf = 0.5 * (1.0 + jnp.tanh(math.sqrt(2.0 / math.pi) * (gate + 0.044715 * (gate * gate * gate))))
    y_ref[...] = (gate * cdf * h).astype(BF16)


def _rglru_call(rx, rg, conv_w, conv_b, wa_bd, ba, wx_bd, bx, lam):
    nt = SEQ // LRU_BLK
    tok = pl.BlockSpec((LRU_BLK, REC_WIDTH), lambda b, t: (b * nt + t, 0))
    vec = pl.BlockSpec((1, REC_WIDTH), lambda b, t: (0, 0))
    mat = pl.BlockSpec((REC_WIDTH, REC_WIDTH), lambda b, t: (0, 0))
    row = lambda v: v.reshape(1, REC_WIDTH)
    return pl.pallas_call(
        _rglru_kernel,
        grid=(BATCH, nt),
        in_specs=[tok, tok, pl.BlockSpec((REC_CONV, REC_WIDTH), lambda b, t: (0, 0)), vec, mat, vec, mat, vec, vec],
        out_specs=tok,
        out_shape=jax.ShapeDtypeStruct((N_TOK, REC_WIDTH), BF16),
        scratch_shapes=[pltpu.VMEM((SUBLANES, REC_WIDTH), F32), pltpu.VMEM((1, REC_WIDTH), F32)],
        compiler_params=_params(2),
        name="rglru",
    )(rx, rg, conv_w, row(conv_b), wa_bd, row(ba), wx_bd, row(bx), row(lam))


def _block_diag(w):
    h, d, _ = w.shape
    eye = jnp.eye(h, dtype=w.dtype)
    return jnp.einsum('hij,hk->hikj', w, eye).reshape(h * d, h * d)


def _diff_kernel(lam_ref, qt_ref, k_ref, vt_ref, g_ref, o_ref,
                 qm_ref, st_ref, mb_ref, p_ref, al_ref, m_ref, acc_ref, *, lambda_init):
    n_grp = qt_ref.shape[2] // LANES
    n_sc = 4 * n_grp
    lv = lam_ref[...]
    lam = (jnp.exp(jnp.sum(lv[0:1] * lv[1:2], axis=-1, keepdims=True))
           - jnp.exp(jnp.sum(lv[2:3] * lv[3:4], axis=-1, keepdims=True)) + lambda_init)

    def query_block(i, carry):
        _diff_query_block(i, lam, qt_ref, k_ref, vt_ref, g_ref, o_ref, qm_ref, st_ref, mb_ref, p_ref, al_ref,
                          m_ref, acc_ref, n_grp=n_grp, n_sc=n_sc, lambda_init=lambda_init)
        return carry

    lax.fori_loop(0, SEQ // ATT_BLK, query_block, 0)


def _diff_query_block(i, lam, qt_ref, k_ref, vt_ref, g_ref, o_ref, qm_ref, st_ref, mb_ref, p_ref, al_ref,
                      m_ref, acc_ref, *, n_grp, n_sc, lambda_init):
    rowq = lax.broadcasted_iota(jnp.int32, (LANES, ATT_BLK), 0)
    for idx in range(n_sc):
        grp, sub = divmod(idx, 4)
        qt = qt_ref[0, i, grp * LANES:(grp + 1) * LANES, :]
        band = (rowq >= sub * DIFF_SUB) & (rowq < (sub + 1) * DIFF_SUB)
        qm_ref[idx] = jnp.where(band, qt, jnp.zeros_like(qt))
        m_ref[idx] = jnp.full((1, ATT_BLK), NEG_BIG, F32)
        acc_ref[idx] = jnp.zeros((ATT_ACC_ROWS, ATT_BLK), F32)
    ones_rows = jnp.ones((ATT_ACC_ROWS - HEAD_DIM, ATT_BLK), BF16)

    def scores(t, slot, masked):
        rows = pl.ds(pl.multiple_of(t * ATT_BLK, ATT_BLK), ATT_BLK)
        if masked:
            kk = lax.broadcasted_iota(jnp.int32, (ATT_BLK, ATT_BLK), 0)
            qq = lax.broadcasted_iota(jnp.int32, (ATT_BLK, ATT_BLK), 1)
            causal = kk <= qq
        for grp in range(n_grp):
            kb = k_ref[0, rows, grp * LANES:(grp + 1) * LANES]
            for idx in range(4 * grp, 4 * grp + 4):
                st = jnp.dot(kb, qm_ref[idx], preferred_element_type=F32)
                if masked:
                    st = jnp.where(causal, st, NEG_BIG)
                st_ref[slot, idx] = st
                mb_ref[slot, idx] = jnp.max(st, axis=0, keepdims=True)

    def softmax(slot):
        for idx in range(n_sc):
            m_old = m_ref[idx]
            m_new = jnp.maximum(m_old, mb_ref[slot, idx])
            p_ref[slot, idx] = jnp.exp2(st_ref[slot, idx] - m_new).astype(BF16)
            al_ref[slot, idx] = jnp.exp2(m_old - m_new)
            m_ref[idx] = m_new

    def values(t, slot):
        for head in range(2 * n_grp):
            vh = jnp.concatenate([vt_ref[0, t, head * HEAD_DIM:(head + 1) * HEAD_DIM, :], ones_rows], axis=0)
            for idx in (2 * head, 2 * head + 1):
                acc_ref[idx] = (al_ref[slot, idx] * acc_ref[idx]
                                + jnp.dot(vh, p_ref[slot, idx], preferred_element_type=F32))

    def advance(t, slot, masked, in_flight=2):
        if in_flight >= 2:
            values(t - 2, slot)
        if in_flight >= 1:
            softmax(1 - slot)
        scores(t, slot, masked)

    def drain(last, last_slot, in_flight=2):
        if in_flight >= 2:
            values(last - 1, 1 - last_slot)
        softmax(last_slot)
        values(last, last_slot)

    def group(u, carry):
        for j in range(ATT_UNROLL):
            advance(ATT_UNROLL * u + j, j % 2, False)
        return carry

    full_groups = i // ATT_UNROLL

    @pl.when(full_groups > 0)
    def _():
        for j in range(ATT_UNROLL):
            advance(j, j % 2, False, in_flight=j)

    lax.fori_loop(1, full_groups, group, 0)

    for rem in range(ATT_UNROLL):
        @pl.when((i % ATT_UNROLL == rem) & (full_groups > 0))
        def _(rem=rem):
            for j in range(rem):
                advance(i - rem + j, j % 2, False)
            advance(i, rem % 2, True)
            drain(i, rem % 2)

        @pl.when(i == rem)
        def _(rem=rem):
            for j in range(rem):
                advance(j, j % 2, False, in_flight=j)
            advance(rem, rem % 2, True, in_flight=rem)
            drain(rem, rem % 2, in_flight=rem + 1)

    out_rows = pl.ds(pl.multiple_of(i * ATT_BLK, ATT_BLK), ATT_BLK)
    for grp in range(n_grp):
        outs = []
        for head in (2 * grp, 2 * grp + 1):
            a0, a1 = acc_ref[2 * head], acc_ref[2 * head + 1]
            o = (a0[:HEAD_DIM] / a0[HEAD_DIM:HEAD_DIM + 1]
                 - lam * (a1[:HEAD_DIM] / a1[HEAD_DIM:HEAD_DIM + 1]))
            ms = jnp.mean(o * o, axis=0, keepdims=True)
            outs.append(o * lax.rsqrt(ms + EPS) * g_ref[...] * (1.0 - lambda_init))
        o_ref[out_rows, grp * LANES:(grp + 1) * LANES] = jnp.concatenate(outs, axis=0).T.astype(BF16)


def _diff_call(lam_vecs, qt, k, vt, subln_g, lambda_init):
    nq = SEQ // ATT_BLK
    width = ATT_GROUPS * LANES
    n_sc = 4 * ATT_GROUPS
    return pl.pallas_call(
        functools.partial(_diff_kernel, lambda_init=lambda_init),
        grid=(BATCH, DIFF_WIDTH // width),
        in_specs=[
            pl.BlockSpec((4, LANES), lambda b, g: (0, 0)),
            pl.BlockSpec((1, nq, width, ATT_BLK), lambda b, g: (b, 0, g, 0)),
            pl.BlockSpec((1, SEQ, width), lambda b, g: (b, 0, g)),
            pl.BlockSpec((1, nq, width, ATT_BLK), lambda b, g: (b, 0, g, 0)),
            pl.BlockSpec((HEAD_DIM, 1), lambda b, g: (0, 0)),
        ],
        out_specs=pl.BlockSpec((SEQ, width), lambda b, g: (b, g)),
        out_shape=jax.ShapeDtypeStruct((N_TOK, DIFF_WIDTH), BF16),
        scratch_shapes=[
            pltpu.VMEM((n_sc, LANES, ATT_BLK), BF16),
            pltpu.VMEM((2, n_sc, ATT_BLK, ATT_BLK), F32),
            pltpu.VMEM((2, n_sc, 1, ATT_BLK), F32),
            pltpu.VMEM((2, n_sc, ATT_BLK, ATT_BLK), BF16),
            pltpu.VMEM((2, n_sc, 1, ATT_BLK), F32),
            pltpu.VMEM((n_sc, 1, ATT_BLK), F32),
            pltpu.VMEM((n_sc, ATT_ACC_ROWS, ATT_BLK), F32),
        ],
        compiler_params=_params(2),
        name="diff_attn",
    )(lam_vecs, qt, k.reshape(BATCH, SEQ, DIFF_WIDTH), vt, subln_g.reshape(HEAD_DIM, 1))


def _ret_kernel(q_ref, k_ref, v_ref, g_ref, dec_ref, kdec_ref, qdec_ref, cdec_ref, bmask_ref, o_ref, s_ref):
    @pl.when(pl.program_id(1) == 0)
    def _():
        s_ref[...] = jnp.zeros_like(s_ref)

    lane = lax.broadcasted_iota(jnp.int32, (RET_BLK, LANES), 1)
    first = lane < HEAD_DIM
    for pair in range(RET_HEADS // 2):
        lanes = slice(pair * LANES, (pair + 1) * LANES)
        state = s_ref[pair]
        for chunk in range(RET_TILE // RET_BLK):
            rows = slice(chunk * RET_BLK, (chunk + 1) * RET_BLK)
            q = q_ref[rows, lanes]
            k = k_ref[rows, lanes]
            v = v_ref[rows, lanes]
            o = jnp.dot(q, state.astype(BF16), preferred_element_type=F32) * qdec_ref[pair]
            for hh in range(2):
                sel = first if hh == 0 else jnp.logical_not(first)
                qm = jnp.where(sel, q, jnp.zeros_like(q))
                sc = lax.dot_general(qm, k, (((1,), (1,)), ((), ())),
                                     preferred_element_type=F32) * dec_ref[pair, hh]
                oi = jnp.dot(sc.astype(BF16), v, preferred_element_type=F32)
                o = o + jnp.where(sel, oi, 0.0)
            kd_t = (k.astype(F32) * kdec_ref[pair]).T.astype(BF16)
            kv = jnp.dot(kd_t, v, preferred_element_type=F32)
            state = state * cdec_ref[pair] + kv * bmask_ref[...]

            o2 = o * o
            ms0 = jnp.sum(jnp.where(first, o2, 0.0), axis=-1, keepdims=True) * (1.0 / HEAD_DIM)
            ms1 = jnp.sum(jnp.where(first, 0.0, o2), axis=-1, keepdims=True) * (1.0 / HEAD_DIM)
            rs = jnp.where(first, lax.rsqrt(ms0 + EPS), lax.rsqrt(ms1 + EPS))
            gate = g_ref[rows, lanes]
            o_ref[rows, lanes] = (gate * _sigmoid(gate) * (o * rs)).astype(BF16)
        s_ref[pair] = state


def _ret_consts():
    c = RET_BLK
    lg = jnp.log1p(-jnp.exp2(-5.0 - jnp.arange(RET_HEADS, dtype=F32)))
    idx = jnp.arange(c, dtype=F32)
    rel = idx[:, None] - idx[None, :]
    decay = jnp.where(rel >= 0, jnp.exp(jnp.maximum(rel, 0.0)[None] * lg[:, None, None]), 0.0)
    k_decay = jnp.exp((c - 1.0 - idx)[None, :] * lg[:, None])
    q_decay = jnp.exp((idx + 1.0)[None, :] * lg[:, None])
    chunk_decay = jnp.exp(c * lg)
    pairs = RET_HEADS // 2
    lanes = lambda t: jnp.repeat(t.reshape(pairs, 2, c).transpose(0, 2, 1), HEAD_DIM, axis=2)
    cdec = jnp.broadcast_to(jnp.repeat(chunk_decay.reshape(pairs, 2), HEAD_DIM, axis=1)[:, :, None],
                            (pairs, LANES, LANES))
    blk = jnp.arange(LANES) // HEAD_DIM
    bmask = (blk[:, None] == blk[None, :]).astype(F32)
    return decay.reshape(pairs, 2, c, c), lanes(k_decay), lanes(q_decay), cdec, bmask


def _ret_call(tq, tk, tv, tg, consts):
    decay, kdec, qdec, cdec, bmask = consts
    nt = SEQ // RET_TILE
    tok = pl.BlockSpec((RET_TILE, RET_WIDTH), lambda b, t: (b * nt + t, 0))
    whole = lambda a: pl.BlockSpec(a.shape, lambda b, t: (0,) * a.ndim)
    return pl.pallas_call(
        _ret_kernel,
        grid=(BATCH, nt),
        in_specs=[tok, tok, tok, tok, whole(decay), whole(kdec), whole(qdec), whole(cdec), whole(bmask)],
        out_specs=tok,
        out_shape=jax.ShapeDtypeStruct((N_TOK, RET_WIDTH), BF16),
        scratch_shapes=[pltpu.VMEM((RET_HEADS // 2, LANES, LANES), F32)],
        compiler_params=_params(2),
        name="retention",
    )(tq, tk, tv, tg, decay, kdec, qdec, cdec, bmask)


def kernel(x, c, positions, norm_ffn1_g, norm_mix_g, norm_ffn2_g, ada_w, ada_b, ffn1_w_in, ffn1_w_out, ffn2_w_in, ffn2_w_out, w_in, w_out, rec_conv_w, rec_conv_b, rec_gate_a_w, rec_gate_a_b, rec_gate_x_w, rec_gate_x_b, rec_lambda, diff_lambda_q1, diff_lambda_k1, diff_lambda_q2, diff_lambda_k2, diff_subln_g, final_norm_g):
    xf = x.reshape(N_TOK, D_MODEL)
    mod = _ada_call(c, ada_w, ada_b).reshape(DEPTH, BATCH, N_MOD, D_MODEL)
    tables = _tables_call(positions)
    ret_consts = _ret_consts()
    bf = lambda w: w.astype(BF16)
    ffn1_wi, ffn1_wo, ffn2_wi, ffn2_wo, mix_wi, mix_wo = map(bf, (ffn1_w_in, ffn1_w_out, ffn2_w_in, ffn2_w_out,
                                                                  w_in, w_out))
    gains = lambda g: g[:, None, :]
    g_ffn1, g_mix, g_ffn2 = gains(norm_ffn1_g), gains(norm_mix_g), gains(norm_ffn2_g)
    for l in range(DEPTH):
        xf = _ffn_call(xf, mod, g_ffn1, ffn1_wi, ffn1_wo, l, mod_row=0)

        rx, rg, qt, k, vt, tq, tk, tv, tg = _proj_call(xf, mod, g_mix, mix_wi, tables, l)
        y_rec = _rglru_call(rx, rg, rec_conv_w[l], rec_conv_b[l],
                            _block_diag(rec_gate_a_w[l]).astype(BF16), rec_gate_a_b[l],
                            _block_diag(rec_gate_x_w[l]).astype(BF16), rec_gate_x_b[l], rec_lambda[l])
        lambda_init = 0.8 - 0.6 * math.exp(-0.3 * l)
        lam_vecs = jnp.stack([diff_lambda_q1[l], diff_lambda_k1[l], diff_lambda_q2[l], diff_lambda_k2[l]])
        lam_vecs = jnp.pad(lam_vecs, ((0, 0), (0, LANES - DIFF_SUB)))
        y_diff = _diff_call(lam_vecs, qt, k, vt, diff_subln_g[l], lambda_init)
        y_ret = _ret_call(tq, tk, tv, tg, ret_consts)

        xf = _ffn_call(xf, mod, g_ffn2, ffn2_wi, ffn2_wo, l, mod_row=6, mix=(y_rec, y_diff, y_ret, mix_wo),
                       final_g=final_norm_g if l == DEPTH - 1 else None)
    return xf.reshape(BATCH, SEQ, D_MODEL)
```

```python
import functools
import math

import jax
import jax.numpy as jnp
from jax import lax
from jax.experimental import pallas as pl
from jax.experimental.pallas import tpu as pltpu

F32 = jnp.float32
BF16 = jnp.bfloat16

D_MODEL = 1024
BATCH = 4
SEQ = 4096
DEPTH = 4
N_TOK = BATCH * SEQ
HEAD_DIM = 64
REC_WIDTH = 384
REC_HEADS = 6
REC_CONV = 4
REC_C = 8.0
DIFF_WIDTH = 384
DIFF_HEADS = 6
DIFF_SUB = 32
ROPE_THETA = 500000.0
ROPE_DIM = 8
RET_WIDTH = 256
RET_HEADS = 4
RET_ROT_BASE = 10000.0
IN_WIDTH = 2944
D_FF = 2816
N_MOD = 9
EPS = 1e-6

LANES = 128
SUBLANES = 8
MXU_DIM = 256
VMEM_LIMIT_BYTES = 56 * 1024 * 1024

TM = 1024
TM_FFN = 1024
FF_CHUNK = MXU_DIM
N_FF_CHUNKS = D_FF // FF_CHUNK
ATT_BLK = 256
ATT_ACC_ROWS = HEAD_DIM + 16
ATT_UNROLL = 4
ATT_GROUPS = 1
RET_BLK = 256
RET_TILE = 2048
LRU_BLK = 1024
TAB_BLK = 1024
NEG_BIG = -1e30


def _sigmoid(x):
    return 1.0 / (1.0 + jnp.exp(-x))


def _split_bf16(x):
    hi = x.astype(BF16)
    return hi, (x - hi.astype(F32)).astype(BF16)


def _norm_mod(x, g, shift, scale):
    ms = jnp.mean(x * x, axis=-1, keepdims=True)
    y = x * lax.rsqrt(ms + EPS) * g
    return y * (1.0 + scale) + shift


def _resident_layer(stacked_shape, layer):
    nd = len(stacked_shape) - 1
    return pl.BlockSpec((None,) + tuple(stacked_shape[1:]), lambda *_: (layer,) + (0,) * nd,
                        pipeline_mode=pl.Buffered(1))


def _mod_spec(layer, rows_per_step):
    steps_per_batch = SEQ // rows_per_step
    return pl.BlockSpec((None, 1, N_MOD, D_MODEL), lambda i: (layer, i // steps_per_batch, 0, 0))


def _gain_spec(layer):
    return pl.BlockSpec((None, 1, D_MODEL), lambda i: (layer, 0, 0))


def _params(n_axes, vmem=VMEM_LIMIT_BYTES, flags=None):
    return pltpu.CompilerParams(dimension_semantics=("arbitrary",) * n_axes, vmem_limit_bytes=vmem, flags=flags)


def _ada_kernel(c_ref, w_ref, b_ref, o_ref):
    c = c_ref[...]
    s = c * _sigmoid(c)
    s_hi, s_lo = _split_bf16(s)
    w_hi, w_lo = _split_bf16(w_ref[0])
    dot = functools.partial(jnp.dot, preferred_element_type=F32)
    o_ref[0] = dot(s_hi, w_hi) + (dot(s_hi, w_lo) + dot(s_lo, w_hi)) + b_ref[0]


def _ada_call(c, ada_w, ada_b):
    tn = 1536
    nmod = N_MOD * D_MODEL
    return pl.pallas_call(
        _ada_kernel,
        grid=(DEPTH, nmod // tn),
        in_specs=[
            pl.BlockSpec((BATCH, D_MODEL), lambda l, j: (0, 0)),
            pl.BlockSpec((1, D_MODEL, tn), lambda l, j: (l, 0, j)),
            pl.BlockSpec((1, 1, tn), lambda l, j: (l, 0, j)),
        ],
        out_specs=pl.BlockSpec((1, BATCH, tn), lambda l, j: (l, 0, j)),
        out_shape=jax.ShapeDtypeStruct((DEPTH, BATCH, nmod), F32),
        compiler_params=_params(2),
        name="ada_mod",
    )(c, ada_w, ada_b.reshape(DEPTH, 1, nmod))


def _tables_kernel(pos_ref, invd_ref, sgnd_ref, invr_ref, sgnr_ref, cd_ref, sd_ref, cr_ref, sr_ref):
    pos = pos_ref[...].astype(F32)
    ang_d = pos * invd_ref[...]
    cd_ref[...] = jnp.cos(ang_d)
    sd_ref[...] = jnp.sin(ang_d) * sgnd_ref[...]
    ang_r = pos * invr_ref[...]
    cr_ref[...] = jnp.cos(ang_r)
    sr_ref[...] = jnp.sin(ang_r) * sgnr_ref[...]


def _rotary_consts():
    lane = jnp.arange(LANES)
    half_d = ROPE_DIM // 2
    inv_d = jnp.power(jnp.float32(ROPE_THETA), -jnp.arange(half_d, dtype=F32) * (2.0 / ROPE_DIM))
    cd = lane % DIFF_SUB
    inv_d_lane = jnp.where(cd < ROPE_DIM, inv_d[cd % half_d], 0.0)
    sgn_d_lane = jnp.where(cd < half_d, -1.0, jnp.where(cd < ROPE_DIM, 1.0, 0.0))
    half_r = HEAD_DIM // 2
    inv_r = jnp.power(jnp.float32(RET_ROT_BASE), -jnp.arange(half_r, dtype=F32) * (2.0 / HEAD_DIM))
    cr = lane % HEAD_DIM
    inv_r_lane = inv_r[cr % half_r]
    sgn_r_lane = jnp.where(cr < half_r, -1.0, 1.0)
    row = lambda v: v.astype(F32).reshape(1, LANES)
    return row(inv_d_lane), row(sgn_d_lane), row(inv_r_lane), row(sgn_r_lane)


def _tables_call(positions):
    consts = _rotary_consts()
    vec = pl.BlockSpec((1, LANES), lambda i: (0, 0))
    tab = pl.BlockSpec((TAB_BLK, LANES), lambda i: (i, 0))
    return pl.pallas_call(
        _tables_kernel,
        grid=(N_TOK // TAB_BLK,),
        in_specs=[pl.BlockSpec((TAB_BLK, 1), lambda i: (i, 0)), vec, vec, vec, vec],
        out_specs=[tab, tab, tab, tab],
        out_shape=[jax.ShapeDtypeStruct((N_TOK, LANES), F32)] * 4,
        compiler_params=_params(1),
        name="rotary_tables",
    )(positions.reshape(N_TOK, 1), *consts)


def _ffn_kernel(*refs, mod_row, with_mix, with_final):
    refs = list(refs)
    x_ref, mod_ref, g_ref = refs[:3]
    del refs[:3]
    if with_mix:
        yr_ref, yd_ref, yt_ref, wm_ref = refs[:4]
        del refs[:4]
    wi_ref, wo_ref = refs[:2]
    del refs[:2]
    if with_final:
        gf_ref = refs.pop(0)
    o_ref, acc_ref = refs

    x = x_ref[...]
    if with_mix:
        y_cat = jnp.concatenate([yr_ref[...], yd_ref[...], yt_ref[...]], axis=1)
        y = jnp.dot(y_cat, wm_ref[...], preferred_element_type=F32)
        x = x + mod_ref[0, mod_row - 1:mod_row, :] * y
    shift = mod_ref[0, mod_row:mod_row + 1, :]
    scale = mod_ref[0, mod_row + 1:mod_row + 2, :]
    gate = mod_ref[0, mod_row + 2:mod_row + 3, :]
    h = _norm_mod(x, g_ref[...], shift, scale).astype(BF16)
    for c in range(N_FF_CHUNKS):
        lo = c * FF_CHUNK
        hg = jnp.dot(h, wi_ref[:, lo:lo + FF_CHUNK], preferred_element_type=F32)
        hu = jnp.dot(h, wi_ref[:, D_FF + lo:D_FF + lo + FF_CHUNK], preferred_element_type=F32)
        a = (hg * _sigmoid(hg) * hu).astype(BF16)
        part = jnp.dot(a, wo_ref[lo:lo + FF_CHUNK, :], preferred_element_type=F32)
        if c == 0:
            acc_ref[...] = part
        else:
            acc_ref[...] += part
    out = x + (0.5 * gate) * acc_ref[...]
    if with_final:
        ms = jnp.mean(out * out, axis=-1, keepdims=True)
        out = out * lax.rsqrt(ms + EPS) * gf_ref[...]
    o_ref[...] = out


def _ffn_call(x, mod, gains, w_in, w_out, layer, mod_row, mix=None, final_g=None):
    tok = lambda w_: pl.BlockSpec((TM_FFN, w_), lambda i: (i, 0))
    operands = [x, mod, gains]
    in_specs = [tok(D_MODEL), _mod_spec(layer, TM_FFN), _gain_spec(layer)]
    if mix is not None:
        operands += list(mix)
        in_specs += [tok(REC_WIDTH), tok(DIFF_WIDTH), tok(RET_WIDTH), _resident_layer(mix[3].shape, layer)]
    operands += [w_in, w_out]
    in_specs += [_resident_layer(w_in.shape, layer), _resident_layer(w_out.shape, layer)]
    if final_g is not None:
        operands.append(final_g.reshape(1, D_MODEL))
        in_specs.append(pl.BlockSpec((1, D_MODEL), lambda i: (0, 0)))
    return pl.pallas_call(
        functools.partial(_ffn_kernel, mod_row=mod_row, with_mix=mix is not None, with_final=final_g is not None),
        grid=(N_TOK // TM_FFN,),
        in_specs=in_specs,
        out_specs=tok(D_MODEL),
        out_shape=jax.ShapeDtypeStruct((N_TOK, D_MODEL), F32),
        scratch_shapes=[pltpu.VMEM((TM_FFN, D_MODEL), F32)],
        compiler_params=_params(1),
        name="ffn_swiglu",
    )(*operands)


def _rope(x, cos, sin, lo_mask, shift_up, shift_dn):
    outs = []
    for j in range(x.shape[1] // LANES):
        xj = x[:, j * LANES:(j + 1) * LANES]
        partner = jnp.where(lo_mask, pltpu.roll(xj, shift_up, 1), pltpu.roll(xj, shift_dn, 1))
        outs.append(xj * cos + partner * sin)
    return jnp.concatenate(outs, axis=1)


def _proj_kernel(x_ref, mod_ref, g_ref, w_ref, cd_ref, sd_ref, cr_ref, sr_ref,
                 rx_ref, rg_ref, qt_ref, k_ref, vt_ref, tq_ref, tk_ref, tv_ref, tg_ref):
    x = x_ref[...]
    shift = mod_ref[0, 3:4, :]
    scale = mod_ref[0, 4:5, :]
    h = _norm_mod(x, g_ref[...], shift, scale).astype(BF16)

    groups = {}
    for lo, hi in ((0, 768), (768, 1536), (1536, IN_WIDTH)):
        groups[lo] = jnp.dot(h, w_ref[:, lo:hi], preferred_element_type=F32)

    def proj(lo, width):
        base = max(b for b in groups if b <= lo)
        return groups[base][:, lo - base:lo - base + width]

    lane = lax.broadcasted_iota(jnp.int32, (TM, LANES), 1)
    lo_d = (lane & (DIFF_SUB - 1)) < (ROPE_DIM // 2)
    lo_r = (lane & (HEAD_DIM - 1)) < (HEAD_DIM // 2)
    cd, sd, cr, sr = cd_ref[...], sd_ref[...], cr_ref[...], sr_ref[...]
    rope_d = lambda v: _rope(v, cd, sd, lo_d, LANES - ROPE_DIM // 2, ROPE_DIM // 2)
    rope_r = lambda v: _rope(v, cr, sr, lo_r, LANES - HEAD_DIM // 2, HEAD_DIM // 2)

    rx_ref[...] = proj(0, REC_WIDTH)
    rg_ref[...] = proj(384, REC_WIDTH)
    q = rope_d(proj(768, DIFF_WIDTH)) * (DIFF_SUB ** -0.5 * math.log2(math.e))
    qt = q.T
    k_ref[...] = rope_d(proj(1152, DIFF_WIDTH)).astype(BF16)
    vt = proj(1536, DIFF_WIDTH).T
    for j in range(TM // ATT_BLK):
        qt_ref[0, j] = qt[:, j * ATT_BLK:(j + 1) * ATT_BLK].astype(BF16)
        vt_ref[0, j] = vt[:, j * ATT_BLK:(j + 1) * ATT_BLK].astype(BF16)
    tq_ref[...] = rope_r(proj(1920, RET_WIDTH)).astype(BF16)
    tk_ref[...] = (rope_r(proj(2176, RET_WIDTH)) * (HEAD_DIM ** -0.5)).astype(BF16)
    tv_ref[...] = proj(2432, RET_WIDTH).astype(BF16)
    tg_ref[...] = proj(2688, RET_WIDTH)


def _proj_call(x, mod, gains, w, tables, layer):
    tiles_per_batch = SEQ // TM
    blocks_per_tile = TM // ATT_BLK
    n_blk = SEQ // ATT_BLK
    tok = lambda w_, : pl.BlockSpec((TM, w_), lambda i: (i, 0))
    tspec = pl.BlockSpec((1, blocks_per_tile, DIFF_WIDTH, ATT_BLK),
                         lambda i: (i // tiles_per_batch, i % tiles_per_batch, 0, 0))
    tshape = jax.ShapeDtypeStruct((BATCH, n_blk, DIFF_WIDTH, ATT_BLK), BF16)
    tok_shape = lambda w_, dt: jax.ShapeDtypeStruct((N_TOK, w_), dt)
    return pl.pallas_call(
        _proj_kernel,
        grid=(N_TOK // TM,),
        in_specs=[
            tok(D_MODEL), _mod_spec(layer, TM), _gain_spec(layer), _resident_layer(w.shape, layer),
            tok(LANES), tok(LANES), tok(LANES), tok(LANES),
        ],
        out_specs=[tok(REC_WIDTH), tok(REC_WIDTH), tspec, tok(DIFF_WIDTH), tspec,
                   tok(RET_WIDTH), tok(RET_WIDTH), tok(RET_WIDTH), tok(RET_WIDTH)],
        out_shape=[tok_shape(REC_WIDTH, F32), tok_shape(REC_WIDTH, F32), tshape, tok_shape(DIFF_WIDTH, BF16), tshape,
                   tok_shape(RET_WIDTH, BF16), tok_shape(RET_WIDTH, BF16), tok_shape(RET_WIDTH, BF16),
                   tok_shape(RET_WIDTH, F32)],
        compiler_params=_params(1),
        name="mixer_in_proj",
    )(x, mod, gains, w, *tables)


def _rglru_kernel(rx_ref, rg_ref, cw_ref, cb_ref, wa_ref, ba_ref, wx_ref, bx_ref, lam_ref,
                  y_ref, xe_ref, h_ref):
    @pl.when(pl.program_id(1) == 0)
    def _():
        xe_ref[...] = jnp.zeros_like(xe_ref)
        h_ref[...] = jnp.zeros_like(h_ref)

    n_grp = LRU_BLK // SUBLANES
    x = rx_ref[...]
    x3 = x.reshape(n_grp, SUBLANES, REC_WIDTH)
    tail = xe_ref[...]
    within = lax.broadcasted_iota(jnp.int32, (n_grp, SUBLANES, REC_WIDTH), 1)
    cw = cw_ref[...]
    xc3 = cb_ref[...] + cw[REC_CONV - 1:REC_CONV, :] * x3
    for k in range(1, REC_CONV):
        rot = pltpu.roll(x3, k, 1)
        before = jnp.concatenate([pltpu.roll(tail, k, 0)[None], rot[:-1]], axis=0)
        xc3 = xc3 + cw[REC_CONV - 1 - k:REC_CONV - k, :] * jnp.where(within >= k, rot, before)
    xe_ref[...] = x[LRU_BLK - SUBLANES:, :]
    xc = xc3.reshape(LRU_BLK, REC_WIDTH)

    xb = xc.astype(BF16)
    r = _sigmoid(jnp.dot(xb, wa_ref[...], preferred_element_type=F32) + ba_ref[...])
    gi = _sigmoid(jnp.dot(xb, wx_ref[...], preferred_element_type=F32) + bx_ref[...])
    nl = -lam_ref[...]
    softplus = jnp.maximum(nl, 0.0) + jnp.log1p(jnp.exp(-jnp.abs(nl)))
    log_a = -REC_C * r * softplus
    a = jnp.exp(log_a)
    one_minus_a2 = -jnp.tanh(log_a) * (a * a + 1.0)
    root = jnp.where(one_minus_a2 > 0.0, one_minus_a2 * lax.rsqrt(one_minus_a2), 0.0)
    b = root * (gi * xc)

    a = a.reshape(n_grp, SUBLANES, REC_WIDTH)
    b = b.reshape(n_grp, SUBLANES, REC_WIDTH)
    s = 1
    while s < SUBLANES:
        valid = within >= s
        a_prev = pltpu.roll(a, s, 1)
        b_prev = pltpu.roll(b, s, 1)
        b = jnp.where(valid, a * b_prev, 0.0) + b
        a = jnp.where(valid, a * a_prev, a)
        s *= 2
    carry = h_ref[...]
    groups = []
    for grp in range(n_grp):
        hg = a[grp] * carry + b[grp]
        groups.append(hg)
        carry = hg[SUBLANES - 1:, :]
    h = jnp.concatenate(groups, axis=0)
    h_ref[...] = carry

    gate = rg_ref[...]
    cdf = 0.5 * (1.0 + jnp.tanh(math.sqrt(2.0 / math.pi) * (gate + 0.044715 * (gate * gate * gate))))
    y_ref[...] = (gate * cdf * h).astype(BF16)


def _rglru_call(rx, rg, conv_w, conv_b, wa_bd, ba, wx_bd, bx, lam):
    nt = SEQ // LRU_BLK
    tok = pl.BlockSpec((LRU_BLK, REC_WIDTH), lambda b, t: (b * nt + t, 0))
    vec = pl.BlockSpec((1, REC_WIDTH), lambda b, t: (0, 0))
    mat = pl.BlockSpec((REC_WIDTH, REC_WIDTH), lambda b, t: (0, 0))
    row = lambda v: v.reshape(1, REC_WIDTH)
    return pl.pallas_call(
        _rglru_kernel,
        grid=(BATCH, nt),
        in_specs=[tok, tok, pl.BlockSpec((REC_CONV, REC_WIDTH), lambda b, t: (0, 0)), vec, mat, vec, mat, vec, vec],
        out_specs=tok,
        out_shape=jax.ShapeDtypeStruct((N_TOK, REC_WIDTH), BF16),
        scratch_shapes=[pltpu.VMEM((SUBLANES, REC_WIDTH), F32), pltpu.VMEM((1, REC_WIDTH), F32)],
        compiler_params=_params(2),
        name="rglru",
    )(rx, rg, conv_w, row(conv_b), wa_bd, row(ba), wx_bd, row(bx), row(lam))


def _block_diag(w):
    h, d, _ = w.shape
    eye = jnp.eye(h, dtype=w.dtype)
    return jnp.einsum('hij,hk->hikj', w, eye).reshape(h * d, h * d)


def _diff_kernel(lam_ref, qt_ref, k_ref, vt_ref, g_ref, o_ref,
                 qm_ref, st_ref, mb_ref, p_ref, al_ref, m_ref, acc_ref, *, lambda_init):
    n_grp = qt_ref.shape[2] // LANES
    n_sc = 4 * n_grp
    lv = lam_ref[...]
    lam = (jnp.exp(jnp.sum(lv[0:1] * lv[1:2], axis=-1, keepdims=True))
           - jnp.exp(jnp.sum(lv[2:3] * lv[3:4], axis=-1, keepdims=True)) + lambda_init)

    def query_block(i, carry):
        _diff_query_block(i, lam, qt_ref, k_ref, vt_ref, g_ref, o_ref, qm_ref, st_ref, mb_ref, p_ref, al_ref,
                          m_ref, acc_ref, n_grp=n_grp, n_sc=n_sc, lambda_init=lambda_init)
        return carry

    lax.fori_loop(0, SEQ // ATT_BLK, query_block, 0)


def _diff_query_block(i, lam, qt_ref, k_ref, vt_ref, g_ref, o_ref, qm_ref, st_ref, mb_ref, p_ref, al_ref,
                      m_ref, acc_ref, *, n_grp, n_sc, lambda_init):
    rowq = lax.broadcasted_iota(jnp.int32, (LANES, ATT_BLK), 0)
    for idx in range(n_sc):
        grp, sub = divmod(idx, 4)
        qt = qt_ref[0, i, grp * LANES:(grp + 1) * LANES, :]
        band = (rowq >= sub * DIFF_SUB) & (rowq < (sub + 1) * DIFF_SUB)
        qm_ref[idx] = jnp.where(band, qt, jnp.zeros_like(qt))
        m_ref[idx] = jnp.full((1, ATT_BLK), NEG_BIG, F32)
        acc_ref[idx] = jnp.zeros((ATT_ACC_ROWS, ATT_BLK), F32)
    ones_rows = jnp.ones((ATT_ACC_ROWS - HEAD_DIM, ATT_BLK), BF16)

    def scores(t, slot, masked):
        rows = pl.ds(pl.multiple_of(t * ATT_BLK, ATT_BLK), ATT_BLK)
        if masked:
            kk = lax.broadcasted_iota(jnp.int32, (ATT_BLK, ATT_BLK), 0)
            qq = lax.broadcasted_iota(jnp.int32, (ATT_BLK, ATT_BLK), 1)
            causal = kk <= qq
        for grp in range(n_grp):
            kb = k_ref[0, rows, grp * LANES:(grp + 1) * LANES]
            for idx in range(4 * grp, 4 * grp + 4):
                st = jnp.dot(kb, qm_ref[idx], preferred_element_type=F32)
                if masked:
                    st = jnp.where(causal, st, NEG_BIG)
                st_ref[slot, idx] = st
                mb_ref[slot, idx] = jnp.max(st, axis=0, keepdims=True)

    def softmax(slot):
        for idx in range(n_sc):
            m_old = m_ref[idx]
            m_new = jnp.maximum(m_old, mb_ref[slot, idx])
            p_ref[slot, idx] = jnp.exp2(st_ref[slot, idx] - m_new).astype(BF16)
            al_ref[slot, idx] = jnp.exp2(m_old - m_new)
            m_ref[idx] = m_new

    def values(t, slot):
        for head in range(2 * n_grp):
            vh = jnp.concatenate([vt_ref[0, t, head * HEAD_DIM:(head + 1) * HEAD_DIM, :], ones_rows], axis=0)
            for idx in (2 * head, 2 * head + 1):
                acc_ref[idx] = (al_ref[slot, idx] * acc_ref[idx]
                                + jnp.dot(vh, p_ref[slot, idx], preferred_element_type=F32))

    def advance(t, slot, masked, in_flight=2):
        if in_flight >= 2:
            values(t - 2, slot)
        if in_flight >= 1:
            softmax(1 - slot)
        scores(t, slot, masked)

    def drain(last, last_slot, in_flight=2):
        if in_flight >= 2:
            values(last - 1, 1 - last_slot)
        softmax(last_slot)
        values(last, last_slot)

    def group(u, carry):
        for j in range(ATT_UNROLL):
            advance(ATT_UNROLL * u + j, j % 2, False)
        return carry

    full_groups = i // ATT_UNROLL

    @pl.when(full_groups > 0)
    def _():
        for j in range(ATT_UNROLL):
            advance(j, j % 2, False, in_flight=j)

    lax.fori_loop(1, full_groups, group, 0)

    for rem in range(ATT_UNROLL):
        @pl.when((i % ATT_UNROLL == rem) & (full_groups > 0))
        def _(rem=rem):
            for j in range(rem):
                advance(i - rem + j, j % 2, False)
            advance(i, rem % 2, True)
            drain(i, rem % 2)

        @pl.when(i == rem)
        def _(rem=rem):
            for j in range(rem):
                advance(j, j % 2, False, in_flight=j)
            advance(rem, rem % 2, True, in_flight=rem)
            drain(rem, rem % 2, in_flight=rem + 1)

    out_rows = pl.ds(pl.multiple_of(i * ATT_BLK, ATT_BLK), ATT_BLK)
    for grp in range(n_grp):
        outs = []
        for head in (2 * grp, 2 * grp + 1):
            a0, a1 = acc_ref[2 * head], acc_ref[2 * head + 1]
            o = (a0[:HEAD_DIM] / a0[HEAD_DIM:HEAD_DIM + 1]
                 - lam * (a1[:HEAD_DIM] / a1[HEAD_DIM:HEAD_DIM + 1]))
            ms = jnp.mean(o * o, axis=0, keepdims=True)
            outs.append(o * lax.rsqrt(ms + EPS) * g_ref[...] * (1.0 - lambda_init))
        o_ref[out_rows, grp * LANES:(grp + 1) * LANES] = jnp.concatenate(outs, axis=0).T.astype(BF16)


def _diff_call(lam_vecs, qt, k, vt, subln_g, lambda_init):
    nq = SEQ // ATT_BLK
    width = ATT_GROUPS * LANES
    n_sc = 4 * ATT_GROUPS
    return pl.pallas_call(
        functools.partial(_diff_kernel, lambda_init=lambda_init),
        grid=(BATCH, DIFF_WIDTH // width),
        in_specs=[
            pl.BlockSpec((4, LANES), lambda b, g: (0, 0)),
            pl.BlockSpec((1, nq, width, ATT_BLK), lambda b, g: (b, 0, g, 0)),
            pl.BlockSpec((1, SEQ, width), lambda b, g: (b, 0, g)),
            pl.BlockSpec((1, nq, width, ATT_BLK), lambda b, g: (b, 0, g, 0)),
            pl.BlockSpec((HEAD_DIM, 1), lambda b, g: (0, 0)),
        ],
        out_specs=pl.BlockSpec((SEQ, width), lambda b, g: (b, g)),
        out_shape=jax.ShapeDtypeStruct((N_TOK, DIFF_WIDTH), BF16),
        scratch_shapes=[
            pltpu.VMEM((n_sc, LANES, ATT_BLK), BF16),
            pltpu.VMEM((2, n_sc, ATT_BLK, ATT_BLK), F32),
            pltpu.VMEM((2, n_sc, 1, ATT_BLK), F32),
            pltpu.VMEM((2, n_sc, ATT_BLK, ATT_BLK), BF16),
            pltpu.VMEM((2, n_sc, 1, ATT_BLK), F32),
            pltpu.VMEM((n_sc, 1, ATT_BLK), F32),
            pltpu.VMEM((n_sc, ATT_ACC_ROWS, ATT_BLK), F32),
        ],
        compiler_params=_params(2),
        name="diff_attn",
    )(lam_vecs, qt, k.reshape(BATCH, SEQ, DIFF_WIDTH), vt, subln_g.reshape(HEAD_DIM, 1))


def _ret_kernel(q_ref, k_ref, v_ref, g_ref, dec_ref, kdec_ref, qdec_ref, cdec_ref, bmask_ref, o_ref, s_ref):
    @pl.when(pl.program_id(1) == 0)
    def _():
        s_ref[...] = jnp.zeros_like(s_ref)

    lane = lax.broadcasted_iota(jnp.int32, (RET_BLK, LANES), 1)
    first = lane < HEAD_DIM
    for pair in range(RET_HEADS // 2):
        lanes = slice(pair * LANES, (pair + 1) * LANES)
        state = s_ref[pair]
        for chunk in range(RET_TILE // RET_BLK):
            rows = slice(chunk * RET_BLK, (chunk + 1) * RET_BLK)
            q = q_ref[rows, lanes]
            k = k_ref[rows, lanes]
            v = v_ref[rows, lanes]
            o = jnp.dot(q, state.astype(BF16), preferred_element_type=F32) * qdec_ref[pair]
            for hh in range(2):
                sel = first if hh == 0 else jnp.logical_not(first)
                qm = jnp.where(sel, q, jnp.zeros_like(q))
                sc = lax.dot_general(qm, k, (((1,), (1,)), ((), ())),
                                     preferred_element_type=F32) * dec_ref[pair, hh]
                oi = jnp.dot(sc.astype(BF16), v, preferred_element_type=F32)
                o = o + jnp.where(sel, oi, 0.0)
            kd_t = (k.astype(F32) * kdec_ref[pair]).T.astype(BF16)
            kv = jnp.dot(kd_t, v, preferred_element_type=F32)
            state = state * cdec_ref[pair] + kv * bmask_ref[...]

            o2 = o * o
            ms0 = jnp.sum(jnp.where(first, o2, 0.0), axis=-1, keepdims=True) * (1.0 / HEAD_DIM)
            ms1 = jnp.sum(jnp.where(first, 0.0, o2), axis=-1, keepdims=True) * (1.0 / HEAD_DIM)
            rs = jnp.where(first, lax.rsqrt(ms0 + EPS), lax.rsqrt(ms1 + EPS))
            gate = g_ref[rows, lanes]
            o_ref[rows, lanes] = (gate * _sigmoid(gate) * (o * rs)).astype(BF16)
        s_ref[pair] = state


def _ret_consts():
    c = RET_BLK
    lg = jnp.log1p(-jnp.exp2(-5.0 - jnp.arange(RET_HEADS, dtype=F32)))
    idx = jnp.arange(c, dtype=F32)
    rel = idx[:, None] - idx[None, :]
    decay = jnp.where(rel >= 0, jnp.exp(jnp.maximum(rel, 0.0)[None] * lg[:, None, None]), 0.0)
    k_decay = jnp.exp((c - 1.0 - idx)[None, :] * lg[:, None])
    q_decay = jnp.exp((idx + 1.0)[None, :] * lg[:, None])
    chunk_decay = jnp.exp(c * lg)
    pairs = RET_HEADS // 2
    lanes = lambda t: jnp.repeat(t.reshape(pairs, 2, c).transpose(0, 2, 1), HEAD_DIM, axis=2)
    cdec = jnp.broadcast_to(jnp.repeat(chunk_decay.reshape(pairs, 2), HEAD_DIM, axis=1)[:, :, None],
                            (pairs, LANES, LANES))
    blk = jnp.arange(LANES) // HEAD_DIM
    bmask = (blk[:, None] == blk[None, :]).astype(F32)
    return decay.reshape(pairs, 2, c, c), lanes(k_decay), lanes(q_decay), cdec, bmask


def _ret_call(tq, tk, tv, tg, consts):
    decay, kdec, qdec, cdec, bmask = consts
    nt = SEQ // RET_TILE
    tok = pl.BlockSpec((RET_TILE, RET_WIDTH), lambda b, t: (b * nt + t, 0))
    whole = lambda a: pl.BlockSpec(a.shape, lambda b, t: (0,) * a.ndim)
    return pl.pallas_call(
        _ret_kernel,
        grid=(BATCH, nt),
        in_specs=[tok, tok, tok, tok, whole(decay), whole(kdec), whole(qdec), whole(cdec), whole(bmask)],
        out_specs=tok,
        out_shape=jax.ShapeDtypeStruct((N_TOK, RET_WIDTH), BF16),
        scratch_shapes=[pltpu.VMEM((RET_HEADS // 2, LANES, LANES), F32)],
        compiler_params=_params(2),
        name="retention",
    )(tq, tk, tv, tg, decay, kdec, qdec, cdec, bmask)


def kernel(x, c, positions, norm_ffn1_g, norm_mix_g, norm_ffn2_g, ada_w, ada_b, ffn1_w_in, ffn1_w_out, ffn2_w_in, ffn2_w_out, w_in, w_out, rec_conv_w, rec_conv_b, rec_gate_a_w, rec_gate_a_b, rec_gate_x_w, rec_gate_x_b, rec_lambda, diff_lambda_q1, diff_lambda_k1, diff_lambda_q2, diff_lambda_k2, diff_subln_g, final_norm_g):
    xf = x.reshape(N_TOK, D_MODEL)
    mod = _ada_call(c, ada_w, ada_b).reshape(DEPTH, BATCH, N_MOD, D_MODEL)
    tables = _tables_call(positions)
    ret_consts = _ret_consts()
    bf = lambda w: w.astype(BF16)
    ffn1_wi, ffn1_wo, ffn2_wi, ffn2_wo, mix_wi, mix_wo = map(bf, (ffn1_w_in, ffn1_w_out, ffn2_w_in, ffn2_w_out,
                                                                  w_in, w_out))
    gains = lambda g: g[:, None, :]
    g_ffn1, g_mix, g_ffn2 = gains(norm_ffn1_g), gains(norm_mix_g), gains(norm_ffn2_g)
    for l in range(DEPTH):
        xf = _ffn_call(xf, mod, g_ffn1, ffn1_wi, ffn1_wo, l, mod_row=0)

        rx, rg, qt, k, vt, tq, tk, tv, tg = _proj_call(xf, mod, g_mix, mix_wi, tables, l)
        y_rec = _rglru_call(rx, rg, rec_conv_w[l], rec_conv_b[l],
                            _block_diag(rec_gate_a_w[l]).astype(BF16), rec_gate_a_b[l],
                            _block_diag(rec_gate_x_w[l]).astype(BF16), rec_gate_x_b[l], rec_lambda[l])
        lambda_init = 0.8 - 0.6 * math.exp(-0.3 * l)
        lam_vecs = jnp.stack([diff_lambda_q1[l], diff_lambda_k1[l], diff_lambda_q2[l], diff_lambda_k2[l]])
        lam_vecs = jnp.pad(lam_vecs, ((0, 0), (0, LANES - DIFF_SUB)))
        y_diff = _diff_call(lam_vecs, qt, k, vt, diff_subln_g[l], lambda_init)
        y_ret = _ret_call(tq, tk, tv, tg, ret_consts)

        xf = _ffn_call(xf, mod, g_ffn2, ffn2_wi, ffn2_wo, l, mod_row=6, mix=(y_rec, y_diff, y_ret, mix_wo),
                       final_g=final_norm_g if l == DEPTH - 1 else None)
    return xf.reshape(BATCH, SEQ, D_MODEL)
```

```python
import functools
import math

import jax
import jax.numpy as jnp
from jax import lax
from jax.experimental import pallas as pl
from jax.experimental.pallas import tpu as pltpu

F32 = jnp.float32
BF16 = jnp.bfloat16

D_MODEL = 1024
BATCH = 4
SEQ = 4096
DEPTH = 4
N_TOK = BATCH * SEQ
HEAD_DIM = 64
REC_WIDTH = 384
REC_HEADS = 6
REC_CONV = 4
REC_C = 8.0
DIFF_WIDTH = 384
DIFF_HEADS = 6
DIFF_SUB = 32
ROPE_THETA = 500000.0
ROPE_DIM = 8
RET_WIDTH = 256
RET_HEADS = 4
RET_ROT_BASE = 10000.0
IN_WIDTH = 2944
D_FF = 2816
N_MOD = 9
EPS = 1e-6

LANES = 128
SUBLANES = 8
MXU_DIM = 256
VMEM_LIMIT_BYTES = 56 * 1024 * 1024

TM = 1024
TM_FFN = 1024
FF_CHUNK = MXU_DIM
N_FF_CHUNKS = D_FF // FF_CHUNK
ATT_BLK = 256
ATT_ACC_ROWS = HEAD_DIM + 16
ATT_UNROLL = 4
ATT_GROUPS = 1
RET_BLK = 256
RET_TILE = 2048
LRU_BLK = 1024
TAB_BLK = 1024
NEG_BIG = -1e30


def _sigmoid(x):
    return 1.0 / (1.0 + jnp.exp(-x))


def _split_bf16(x):
    hi = x.astype(BF16)
    return hi, (x - hi.astype(F32)).astype(BF16)


def _norm_mod(x, g, shift, scale):
    ms = jnp.mean(x * x, axis=-1, keepdims=True)
    y = x * lax.rsqrt(ms + EPS) * g
    return y * (1.0 + scale) + shift


def _resident_layer(stacked_shape, layer):
    nd = len(stacked_shape) - 1
    return pl.BlockSpec((None,) + tuple(stacked_shape[1:]), lambda *_: (layer,) + (0,) * nd,
                        pipeline_mode=pl.Buffered(1))


def _mod_spec(layer, rows_per_step):
    steps_per_batch = SEQ // rows_per_step
    return pl.BlockSpec((None, 1, N_MOD, D_MODEL), lambda i: (layer, i // steps_per_batch, 0, 0))


def _gain_spec(layer):
    return pl.BlockSpec((None, 1, D_MODEL), lambda i: (layer, 0, 0))


def _params(n_axes, vmem=VMEM_LIMIT_BYTES, flags=None):
    return pltpu.CompilerParams(dimension_semantics=("arbitrary",) * n_axes, vmem_limit_bytes=vmem, flags=flags)


def _ada_kernel(c_ref, w_ref, b_ref, o_ref):
    c = c_ref[...]
    s = c * _sigmoid(c)
    s_hi, s_lo = _split_bf16(s)
    w_hi, w_lo = _split_bf16(w_ref[0])
    dot = functools.partial(jnp.dot, preferred_element_type=F32)
    o_ref[0] = dot(s_hi, w_hi) + (dot(s_hi, w_lo) + dot(s_lo, w_hi)) + b_ref[0]


def _ada_call(c, ada_w, ada_b):
    tn = 1536
    nmod = N_MOD * D_MODEL
    return pl.pallas_call(
        _ada_kernel,
        grid=(DEPTH, nmod // tn),
        in_specs=[
            pl.BlockSpec((BATCH, D_MODEL), lambda l, j: (0, 0)),
            pl.BlockSpec((1, D_MODEL, tn), lambda l, j: (l, 0, j)),
            pl.BlockSpec((1, 1, tn), lambda l, j: (l, 0, j)),
        ],
        out_specs=pl.BlockSpec((1, BATCH, tn), lambda l, j: (l, 0, j)),
        out_shape=jax.ShapeDtypeStruct((DEPTH, BATCH, nmod), F32),
        compiler_params=_params(2),
        name="ada_mod",
    )(c, ada_w, ada_b.reshape(DEPTH, 1, nmod))


def _tables_kernel(pos_ref, inv_ref, sel_ref, one_ref, cd_ref, sd_ref, cr_ref, sr_ref):
    ang = pos_ref[...].astype(F32) * inv_ref[...]
    dot = functools.partial(jnp.dot, preferred_element_type=F32)
    spread = lambda parts, m: dot(parts[0], sel_ref[m]) + dot(parts[1], sel_ref[m])
    cos, sin = _split_bf16(jnp.cos(ang)), _split_bf16(jnp.sin(ang))
    cd_ref[...] = spread(cos, 0) + one_ref[...]
    sd_ref[...] = spread(sin, 1)
    cr_ref[...] = spread(cos, 2)
    sr_ref[...] = spread(sin, 3)


def _rotary_consts():
    lane = jnp.arange(LANES)
    half_d, half_r = ROPE_DIM // 2, HEAD_DIM // 2
    inv_d = jnp.power(jnp.float32(ROPE_THETA), -jnp.arange(half_d, dtype=F32) * (2.0 / ROPE_DIM))
    inv_r = jnp.power(jnp.float32(RET_ROT_BASE), -jnp.arange(half_r, dtype=F32) * (2.0 / HEAD_DIM))
    inv = jnp.zeros((LANES,), F32).at[:half_r].set(inv_r).at[half_r:half_r + half_d].set(inv_d)
    cd = lane % DIFF_SUB
    src_d = half_r + cd % half_d
    on_d = cd < ROPE_DIM
    sgn_d = jnp.where(cd < half_d, -1.0, 1.0)
    cr = lane % HEAD_DIM
    src_r = cr % half_r
    sgn_r = jnp.where(cr < half_r, -1.0, 1.0)
    pick = lambda src, weight: jnp.where(lane[:, None] == src[None, :], weight[None, :], 0.0)
    ones = jnp.ones((LANES,), F32)
    sel = jnp.stack([pick(src_d, jnp.where(on_d, 1.0, 0.0)), pick(src_d, jnp.where(on_d, sgn_d, 0.0)),
                     pick(src_r, ones), pick(src_r, sgn_r)]).astype(BF16)
    return inv.reshape(1, LANES), sel, jnp.where(on_d, 0.0, 1.0).astype(F32).reshape(1, LANES)


def _tables_call(positions):
    inv, sel, one_d = _rotary_consts()
    vec = pl.BlockSpec((1, LANES), lambda i: (0, 0))
    tab = pl.BlockSpec((TAB_BLK, LANES), lambda i: (i, 0))
    return pl.pallas_call(
        _tables_kernel,
        grid=(N_TOK // TAB_BLK,),
        in_specs=[pl.BlockSpec((TAB_BLK, 1), lambda i: (i, 0)), vec,
                  pl.BlockSpec(sel.shape, lambda i: (0, 0, 0)), vec],
        out_specs=[tab, tab, tab, tab],
        out_shape=[jax.ShapeDtypeStruct((N_TOK, LANES), F32)] * 4,
        compiler_params=_params(1),
        name="rotary_tables",
    )(positions.reshape(N_TOK, 1), inv, sel, one_d)


def _ffn_kernel(*refs, mod_row, with_mix, with_final):
    refs = list(refs)
    x_ref, mod_ref, g_ref = refs[:3]
    del refs[:3]
    if with_mix:
        yr_ref, yd_ref, yt_ref, wm_ref = refs[:4]
        del refs[:4]
    wi_ref, wo_ref = refs[:2]
    del refs[:2]
    if with_final:
        gf_ref = refs.pop(0)
    o_ref, acc_ref = refs

    x = x_ref[...]
    if with_mix:
        y_cat = jnp.concatenate([yr_ref[...], yd_ref[...], yt_ref[...]], axis=1)
        y = jnp.dot(y_cat, wm_ref[...], preferred_element_type=F32)
        x = x + mod_ref[0, mod_row - 1:mod_row, :] * y
    shift = mod_ref[0, mod_row:mod_row + 1, :]
    scale = mod_ref[0, mod_row + 1:mod_row + 2, :]
    gate = mod_ref[0, mod_row + 2:mod_row + 3, :]
    h = _norm_mod(x, g_ref[...], shift, scale).astype(BF16)
    for c in range(N_FF_CHUNKS):
        lo = c * FF_CHUNK
        hg = jnp.dot(h, wi_ref[:, lo:lo + FF_CHUNK], preferred_element_type=F32)
        hu = jnp.dot(h, wi_ref[:, D_FF + lo:D_FF + lo + FF_CHUNK], preferred_element_type=F32)
        a = (hg * _sigmoid(hg) * hu).astype(BF16)
        part = jnp.dot(a, wo_ref[lo:lo + FF_CHUNK, :], preferred_element_type=F32)
        if c == 0:
            acc_ref[...] = part
        else:
            acc_ref[...] += part
    out = x + (0.5 * gate) * acc_ref[...]
    if with_final:
        ms = jnp.mean(out * out, axis=-1, keepdims=True)
        out = out * lax.rsqrt(ms + EPS) * gf_ref[...]
    o_ref[...] = out


def _ffn_call(x, mod, gains, w_in, w_out, layer, mod_row, mix=None, final_g=None):
    tok = lambda w_: pl.BlockSpec((TM_FFN, w_), lambda i: (i, 0))
    operands = [x, mod, gains]
    in_specs = [tok(D_MODEL), _mod_spec(layer, TM_FFN), _gain_spec(layer)]
    if mix is not None:
        operands += list(mix)
        in_specs += [tok(REC_WIDTH), tok(DIFF_WIDTH), tok(RET_WIDTH), _resident_layer(mix[3].shape, layer)]
    operands += [w_in, w_out]
    in_specs += [_resident_layer(w_in.shape, layer), _resident_layer(w_out.shape, layer)]
    if final_g is not None:
        operands.append(final_g.reshape(1, D_MODEL))
        in_specs.append(pl.BlockSpec((1, D_MODEL), lambda i: (0, 0)))
    return pl.pallas_call(
        functools.partial(_ffn_kernel, mod_row=mod_row, with_mix=mix is not None, with_final=final_g is not None),
        grid=(N_TOK // TM_FFN,),
        in_specs=in_specs,
        out_specs=tok(D_MODEL),
        out_shape=jax.ShapeDtypeStruct((N_TOK, D_MODEL), F32),
        scratch_shapes=[pltpu.VMEM((TM_FFN, D_MODEL), F32)],
        compiler_params=_params(1),
        name="ffn_swiglu",
    )(*operands)


def _rope(x, cos, sin, lo_mask, shift_up, shift_dn):
    outs = []
    for j in range(x.shape[1] // LANES):
        xj = x[:, j * LANES:(j + 1) * LANES]
        partner = jnp.where(lo_mask, pltpu.roll(xj, shift_up, 1), pltpu.roll(xj, shift_dn, 1))
        outs.append(xj * cos + partner * sin)
    return jnp.concatenate(outs, axis=1)


def _proj_kernel(x_ref, mod_ref, g_ref, w_ref, cd_ref, sd_ref, cr_ref, sr_ref,
                 rx_ref, rg_ref, qt_ref, k_ref, vt_ref, tq_ref, tk_ref, tv_ref, tg_ref):
    x = x_ref[...]
    shift = mod_ref[0, 3:4, :]
    scale = mod_ref[0, 4:5, :]
    h = _norm_mod(x, g_ref[...], shift, scale).astype(BF16)

    groups = {}
    for lo, hi in ((0, 768), (768, 1536), (1536, IN_WIDTH)):
        groups[lo] = jnp.dot(h, w_ref[:, lo:hi], preferred_element_type=F32)

    def proj(lo, width):
        base = max(b for b in groups if b <= lo)
        return groups[base][:, lo - base:lo - base + width]

    lane = lax.broadcasted_iota(jnp.int32, (TM, LANES), 1)
    lo_d = (lane & (DIFF_SUB - 1)) < (ROPE_DIM // 2)
    lo_r = (lane & (HEAD_DIM - 1)) < (HEAD_DIM // 2)
    cd, sd, cr, sr = cd_ref[...], sd_ref[...], cr_ref[...], sr_ref[...]
    rope_d = lambda v: _rope(v, cd, sd, lo_d, LANES - ROPE_DIM // 2, ROPE_DIM // 2)
    rope_r = lambda v: _rope(v, cr, sr, lo_r, LANES - HEAD_DIM // 2, HEAD_DIM // 2)

    rx_ref[...] = proj(0, REC_WIDTH)
    rg_ref[...] = proj(384, REC_WIDTH)
    q = rope_d(proj(768, DIFF_WIDTH)) * (DIFF_SUB ** -0.5 * math.log2(math.e))
    qt = q.T
    k_ref[...] = rope_d(proj(1152, DIFF_WIDTH)).astype(BF16)
    vt = proj(1536, DIFF_WIDTH).T
    for j in range(TM // ATT_BLK):
        qt_ref[0, j] = qt[:, j * ATT_BLK:(j + 1) * ATT_BLK].astype(BF16)
        vt_ref[0, j] = vt[:, j * ATT_BLK:(j + 1) * ATT_BLK].astype(BF16)
    tq_ref[...] = rope_r(proj(1920, RET_WIDTH)).astype(BF16)
    tk_ref[...] = (rope_r(proj(2176, RET_WIDTH)) * (HEAD_DIM ** -0.5)).astype(BF16)
    tv_ref[...] = proj(2432, RET_WIDTH).astype(BF16)
    tg_ref[...] = proj(2688, RET_WIDTH)


def _proj_call(x, mod, gains, w, tables, layer):
    tiles_per_batch = SEQ // TM
    blocks_per_tile = TM // ATT_BLK
    n_blk = SEQ // ATT_BLK
    tok = lambda w_, : pl.BlockSpec((TM, w_), lambda i: (i, 0))
    tspec = pl.BlockSpec((1, blocks_per_tile, DIFF_WIDTH, ATT_BLK),
                         lambda i: (i // tiles_per_batch, i % tiles_per_batch, 0, 0))
    tshape = jax.ShapeDtypeStruct((BATCH, n_blk, DIFF_WIDTH, ATT_BLK), BF16)
    tok_shape = lambda w_, dt: jax.ShapeDtypeStruct((N_TOK, w_), dt)
    return pl.pallas_call(
        _proj_kernel,
        grid=(N_TOK // TM,),
        in_specs=[
            tok(D_MODEL), _mod_spec(layer, TM), _gain_spec(layer), _resident_layer(w.shape, layer),
            tok(LANES), tok(LANES), tok(LANES), tok(LANES),
        ],
        out_specs=[tok(REC_WIDTH), tok(REC_WIDTH), tspec, tok(DIFF_WIDTH), tspec,
                   tok(RET_WIDTH), tok(RET_WIDTH), tok(RET_WIDTH), tok(RET_WIDTH)],
        out_shape=[tok_shape(REC_WIDTH, F32), tok_shape(REC_WIDTH, F32), tshape, tok_shape(DIFF_WIDTH, BF16), tshape,
                   tok_shape(RET_WIDTH, BF16), tok_shape(RET_WIDTH, BF16), tok_shape(RET_WIDTH, BF16),
                   tok_shape(RET_WIDTH, F32)],
        compiler_params=_params(1),
        name="mixer_in_proj",
    )(x, mod, gains, w, *tables)


def _rglru_kernel(rx_ref, rg_ref, cw_ref, cb_ref, wa_ref, ba_ref, wx_ref, bx_ref, lam_ref,
                  y_ref, xe_ref, h_ref):
    @pl.when(pl.program_id(1) == 0)
    def _():
        xe_ref[...] = jnp.zeros_like(xe_ref)
        h_ref[...] = jnp.zeros_like(h_ref)

    n_grp = LRU_BLK // SUBLANES
    x = rx_ref[...]
    x3 = x.reshape(n_grp, SUBLANES, REC_WIDTH)
    tail = xe_ref[...]
    within = lax.broadcasted_iota(jnp.int32, (n_grp, SUBLANES, REC_WIDTH), 1)
    cw = cw_ref[...]
    xc3 = cb_ref[...] + cw[REC_CONV - 1:REC_CONV, :] * x3
    for k in range(1, REC_CONV):
        rot = pltpu.roll(x3, k, 1)
        before = jnp.concatenate([pltpu.roll(tail, k, 0)[None], rot[:-1]], axis=0)
        xc3 = xc3 + cw[REC_CONV - 1 - k:REC_CONV - k, :] * jnp.where(within >= k, rot, before)
    xe_ref[...] = x[LRU_BLK - SUBLANES:, :]
    xc = xc3.reshape(LRU_BLK, REC_WIDTH)

    xb = xc.astype(BF16)
    r = _sigmoid(jnp.dot(xb, wa_ref[...], preferred_element_type=F32) + ba_ref[...])
    gi = _sigmoid(jnp.dot(xb, wx_ref[...], preferred_element_type=F32) + bx_ref[...])
    nl = -lam_ref[...]
    softplus = jnp.maximum(nl, 0.0) + jnp.log1p(jnp.exp(-jnp.abs(nl)))
    log_a = -REC_C * r * softplus
    a = jnp.exp(log_a)
    one_minus_a2 = -jnp.tanh(log_a) * (a * a + 1.0)
    root = jnp.where(one_minus_a2 > 0.0, one_minus_a2 * lax.rsqrt(one_minus_a2), 0.0)
    b = root * (gi * xc)

    a = a.reshape(n_grp, SUBLANES, REC_WIDTH)
    b = b.reshape(n_grp, SUBLANES, REC_WIDTH)
    s = 1
    while s < SUBLANES:
        valid = within >= s
        a_prev = pltpu.roll(a, s, 1)
        b_prev = pltpu.roll(b, s, 1)
        b = jnp.where(valid, a * b_prev, 0.0) + b
        a = jnp.where(valid, a * a_prev, a)
        s *= 2
    carry = h_ref[...]
    groups = []
    for grp in range(n_grp):
        hg = a[grp] * carry + b[grp]
        groups.append(hg)
        carry = hg[SUBLANES - 1:, :]
    h = jnp.concatenate(groups, axis=0)
    h_ref[...] = carry

    gate = rg_ref[...]
    cdf = 0.5 * (1.0 + jnp.tanh(math.sqrt(2.0 / math.pi) * (gate + 0.044715 * (gate * gate * gate))))
    y_ref[...] = (gate * cdf * h).astype(BF16)


def _rglru_call(rx, rg, conv_w, conv_b, wa_bd, ba, wx_bd, bx, lam):
    nt = SEQ // LRU_BLK
    tok = pl.BlockSpec((LRU_BLK, REC_WIDTH), lambda b, t: (b * nt + t, 0))
    vec = pl.BlockSpec((1, REC_WIDTH), lambda b, t: (0, 0))
    mat = pl.BlockSpec((REC_WIDTH, REC_WIDTH), lambda b, t: (0, 0))
    row = lambda v: v.reshape(1, REC_WIDTH)
    return pl.pallas_call(
        _rglru_kernel,
        grid=(BATCH, nt),
        in_specs=[tok, tok, pl.BlockSpec((REC_CONV, REC_WIDTH), lambda b, t: (0, 0)), vec, mat, vec, mat, vec, vec],
        out_specs=tok,
        out_shape=jax.ShapeDtypeStruct((N_TOK, REC_WIDTH), BF16),
        scratch_shapes=[pltpu.VMEM((SUBLANES, REC_WIDTH), F32), pltpu.VMEM((1, REC_WIDTH), F32)],
        compiler_params=_params(2),
        name="rglru",
    )(rx, rg, conv_w, row(conv_b), wa_bd, row(ba), wx_bd, row(bx), row(lam))


def _block_diag(w):
    h, d, _ = w.shape
    eye = jnp.eye(h, dtype=w.dtype)
    return jnp.einsum('hij,hk->hikj', w, eye).reshape(h * d, h * d)


def _diff_kernel(lam_ref, qt_ref, k_ref, vt_ref, g_ref, o_ref,
                 qm_ref, st_ref, mb_ref, p_ref, al_ref, m_ref, acc_ref, *, lambda_init):
    n_grp = qt_ref.shape[2] // LANES
    n_sc = 4 * n_grp
    lv = lam_ref[...]
    lam = (jnp.exp(jnp.sum(lv[0:1] * lv[1:2], axis=-1, keepdims=True))
           - jnp.exp(jnp.sum(lv[2:3] * lv[3:4], axis=-1, keepdims=True)) + lambda_init)

    def query_block(i, carry):
        _diff_query_block(i, lam, qt_ref, k_ref, vt_ref, g_ref, o_ref, qm_ref, st_ref, mb_ref, p_ref, al_ref,
                          m_ref, acc_ref, n_grp=n_grp, n_sc=n_sc, lambda_init=lambda_init)
        return carry

    lax.fori_loop(0, SEQ // ATT_BLK, query_block, 0)


def _diff_query_block(i, lam, qt_ref, k_ref, vt_ref, g_ref, o_ref, qm_ref, st_ref, mb_ref, p_ref, al_ref,
                      m_ref, acc_ref, *, n_grp, n_sc, lambda_init):
    rowq = lax.broadcasted_iota(jnp.int32, (LANES, ATT_BLK), 0)
    for idx in range(n_sc):
        grp, sub = divmod(idx, 4)
        qt = qt_ref[0, i, grp * LANES:(grp + 1) * LANES, :]
        band = (rowq >= sub * DIFF_SUB) & (rowq < (sub + 1) * DIFF_SUB)
        qm_ref[idx] = jnp.where(band, qt, jnp.zeros_like(qt))
        m_ref[idx] = jnp.full((1, ATT_BLK), NEG_BIG, F32)
        acc_ref[idx] = jnp.zeros((ATT_ACC_ROWS, ATT_BLK), F32)
    ones_rows = jnp.ones((ATT_ACC_ROWS - HEAD_DIM, ATT_BLK), BF16)

    def scores(t, slot, masked):
        rows = pl.ds(pl.multiple_of(t * ATT_BLK, ATT_BLK), ATT_BLK)
        if masked:
            kk = lax.broadcasted_iota(jnp.int32, (ATT_BLK, ATT_BLK), 0)
            qq = lax.broadcasted_iota(jnp.int32, (ATT_BLK, ATT_BLK), 1)
            causal = kk <= qq
        for grp in range(n_grp):
            kb = k_ref[0, rows, grp * LANES:(grp + 1) * LANES]
            for idx in range(4 * grp, 4 * grp + 4):
                st = jnp.dot(kb, qm_ref[idx], preferred_element_type=F32)
                if masked:
                    st = jnp.where(causal, st, NEG_BIG)
                st_ref[slot, idx] = st
                mb_ref[slot, idx] = jnp.max(st, axis=0, keepdims=True)

    def softmax(slot):
        for idx in range(n_sc):
            m_old = m_ref[idx]
            m_new = jnp.maximum(m_old, mb_ref[slot, idx])
            p_ref[slot, idx] = jnp.exp2(st_ref[slot, idx] - m_new).astype(BF16)
            al_ref[slot, idx] = jnp.exp2(m_old - m_new)
            m_ref[idx] = m_new

    def values(t, slot):
        for head in range(2 * n_grp):
            vh = jnp.concatenate([vt_ref[0, t, head * HEAD_DIM:(head + 1) * HEAD_DIM, :], ones_rows], axis=0)
            for idx in (2 * head, 2 * head + 1):
                acc_ref[idx] = (al_ref[slot, idx] * acc_ref[idx]
                                + jnp.dot(vh, p_ref[slot, idx], preferred_element_type=F32))

    def advance(t, slot, masked, in_flight=2):
        if in_flight >= 2:
            values(t - 2, slot)
        if in_flight >= 1:
            softmax(1 - slot)
        scores(t, slot, masked)

    def drain(last, last_slot, in_flight=2):
        if in_flight >= 2:
            values(last - 1, 1 - last_slot)
        softmax(last_slot)
        values(last, last_slot)

    def group(u, carry):
        for j in range(ATT_UNROLL):
            advance(ATT_UNROLL * u + j, j % 2, False)
        return carry

    full_groups = i // ATT_UNROLL

    @pl.when(full_groups > 0)
    def _():
        for j in range(ATT_UNROLL):
            advance(j, j % 2, False, in_flight=j)

    lax.fori_loop(1, full_groups, group, 0)

    for rem in range(ATT_UNROLL):
        @pl.when((i % ATT_UNROLL == rem) & (full_groups > 0))
        def _(rem=rem):
            for j in range(rem):
                advance(i - rem + j, j % 2, False)
            advance(i, rem % 2, True)
            drain(i, rem % 2)

        @pl.when(i == rem)
        def _(rem=rem):
            for j in range(rem):
                advance(j, j % 2, False, in_flight=j)
            advance(rem, rem % 2, True, in_flight=rem)
            drain(rem, rem % 2, in_flight=rem + 1)

    out_rows = pl.ds(pl.multiple_of(i * ATT_BLK, ATT_BLK), ATT_BLK)
    for grp in range(n_grp):
        outs = []
        for head in (2 * grp, 2 * grp + 1):
            a0, a1 = acc_ref[2 * head], acc_ref[2 * head + 1]
            o = (a0[:HEAD_DIM] / a0[HEAD_DIM:HEAD_DIM + 1]
                 - lam * (a1[:HEAD_DIM] / a1[HEAD_DIM:HEAD_DIM + 1]))
            ms = jnp.mean(o * o, axis=0, keepdims=True)
            outs.append(o * lax.rsqrt(ms + EPS) * g_ref[...] * (1.0 - lambda_init))
        o_ref[out_rows, grp * LANES:(grp + 1) * LANES] = jnp.concatenate(outs, axis=0).T.astype(BF16)


def _diff_call(lam_vecs, qt, k, vt, subln_g, lambda_init):
    nq = SEQ // ATT_BLK
    width = ATT_GROUPS * LANES
    n_sc = 4 * ATT_GROUPS
    return pl.pallas_call(
        functools.partial(_diff_kernel, lambda_init=lambda_init),
        grid=(BATCH, DIFF_WIDTH // width),
        in_specs=[
            pl.BlockSpec((4, LANES), lambda b, g: (0, 0)),
            pl.BlockSpec((1, nq, width, ATT_BLK), lambda b, g: (b, 0, g, 0)),
            pl.BlockSpec((1, SEQ, width), lambda b, g: (b, 0, g)),
            pl.BlockSpec((1, nq, width, ATT_BLK), lambda b, g: (b, 0, g, 0)),
            pl.BlockSpec((HEAD_DIM, 1), lambda b, g: (0, 0)),
        ],
        out_specs=pl.BlockSpec((SEQ, width), lambda b, g: (b, g)),
        out_shape=jax.ShapeDtypeStruct((N_TOK, DIFF_WIDTH), BF16),
        scratch_shapes=[
            pltpu.VMEM((n_sc, LANES, ATT_BLK), BF16),
            pltpu.VMEM((2, n_sc, ATT_BLK, ATT_BLK), F32),
            pltpu.VMEM((2, n_sc, 1, ATT_BLK), F32),
            pltpu.VMEM((2, n_sc, ATT_BLK, ATT_BLK), BF16),
            pltpu.VMEM((2, n_sc, 1, ATT_BLK), F32),
            pltpu.VMEM((n_sc, 1, ATT_BLK), F32),
            pltpu.VMEM((n_sc, ATT_ACC_ROWS, ATT_BLK), F32),
        ],
        compiler_params=_params(2),
        name="diff_attn",
    )(lam_vecs, qt, k.reshape(BATCH, SEQ, DIFF_WIDTH), vt, subln_g.reshape(HEAD_DIM, 1))


def _ret_kernel(q_ref, k_ref, v_ref, g_ref, dec_ref, kdec_ref, qdec_ref, cdec_ref, bmask_ref, o_ref, s_ref):
    @pl.when(pl.program_id(1) == 0)
    def _():
        s_ref[...] = jnp.zeros_like(s_ref)

    lane = lax.broadcasted_iota(jnp.int32, (RET_BLK, LANES), 1)
    first = lane < HEAD_DIM
    for pair in range(RET_HEADS // 2):
        lanes = slice(pair * LANES, (pair + 1) * LANES)
        state = s_ref[pair]
        for chunk in range(RET_TILE // RET_BLK):
            rows = slice(chunk * RET_BLK, (chunk + 1) * RET_BLK)
            q = q_ref[rows, lanes]
            k = k_ref[rows, lanes]
            v = v_ref[rows, lanes]
            o = jnp.dot(q, state.astype(BF16), preferred_element_type=F32) * qdec_ref[pair]
            for hh in range(2):
                sel = first if hh == 0 else jnp.logical_not(first)
                qm = jnp.where(sel, q, jnp.zeros_like(q))
                sc = lax.dot_general(qm, k, (((1,), (1,)), ((), ())),
                                     preferred_element_type=F32) * dec_ref[pair, hh]
                oi = jnp.dot(sc.astype(BF16), v, preferred_element_type=F32)
                o = o + jnp.where(sel, oi, 0.0)
            kd_t = (k.astype(F32) * kdec_ref[pair]).T.astype(BF16)
            kv = jnp.dot(kd_t, v, preferred_element_type=F32)
            state = state * cdec_ref[pair] + kv * bmask_ref[...]

            o2 = o * o
            ms0 = jnp.sum(jnp.where(first, o2, 0.0), axis=-1, keepdims=True) * (1.0 / HEAD_DIM)
            ms1 = jnp.sum(jnp.where(first, 0.0, o2), axis=-1, keepdims=True) * (1.0 / HEAD_DIM)
            rs = jnp.where(first, lax.rsqrt(ms0 + EPS), lax.rsqrt(ms1 + EPS))
            gate = g_ref[rows, lanes]
            o_ref[rows, lanes] = (gate * _sigmoid(gate) * (o * rs)).astype(BF16)
        s_ref[pair] = state


def _ret_consts():
    c = RET_BLK
    lg = jnp.log1p(-jnp.exp2(-5.0 - jnp.arange(RET_HEADS, dtype=F32)))
    idx = jnp.arange(c, dtype=F32)
    rel = idx[:, None] - idx[None, :]
    decay = jnp.where(rel >= 0, jnp.exp(jnp.maximum(rel, 0.0)[None] * lg[:, None, None]), 0.0)
    k_decay = jnp.exp((c - 1.0 - idx)[None, :] * lg[:, None])
    q_decay = jnp.exp((idx + 1.0)[None, :] * lg[:, None])
    chunk_decay = jnp.exp(c * lg)
    pairs = RET_HEADS // 2
    lanes = lambda t: jnp.repeat(t.reshape(pairs, 2, c).transpose(0, 2, 1), HEAD_DIM, axis=2)
    cdec = jnp.broadcast_to(jnp.repeat(chunk_decay.reshape(pairs, 2), HEAD_DIM, axis=1)[:, :, None],
                            (pairs, LANES, LANES))
    blk = jnp.arange(LANES) // HEAD_DIM
    bmask = (blk[:, None] == blk[None, :]).astype(F32)
    return decay.reshape(pairs, 2, c, c), lanes(k_decay), lanes(q_decay), cdec, bmask


def _ret_call(tq, tk, tv, tg, consts):
    decay, kdec, qdec, cdec, bmask = consts
    nt = SEQ // RET_TILE
    tok = pl.BlockSpec((RET_TILE, RET_WIDTH), lambda b, t: (b * nt + t, 0))
    whole = lambda a: pl.BlockSpec(a.shape, lambda b, t: (0,) * a.ndim)
    return pl.pallas_call(
        _ret_kernel,
        grid=(BATCH, nt),
        in_specs=[tok, tok, tok, tok, whole(decay), whole(kdec), whole(qdec), whole(cdec), whole(bmask)],
        out_specs=tok,
        out_shape=jax.ShapeDtypeStruct((N_TOK, RET_WIDTH), BF16),
        scratch_shapes=[pltpu.VMEM((RET_HEADS // 2, LANES, LANES), F32)],
        compiler_params=_params(2),
        name="retention",
    )(tq, tk, tv, tg, decay, kdec, qdec, cdec, bmask)


def kernel(x, c, positions, norm_ffn1_g, norm_mix_g, norm_ffn2_g, ada_w, ada_b, ffn1_w_in, ffn1_w_out, ffn2_w_in, ffn2_w_out, w_in, w_out, rec_conv_w, rec_conv_b, rec_gate_a_w, rec_gate_a_b, rec_gate_x_w, rec_gate_x_b, rec_lambda, diff_lambda_q1, diff_lambda_k1, diff_lambda_q2, diff_lambda_k2, diff_subln_g, final_norm_g):
    xf = x.reshape(N_TOK, D_MODEL)
    mod = _ada_call(c, ada_w, ada_b).reshape(DEPTH, BATCH, N_MOD, D_MODEL)
    tables = _tables_call(positions)
    ret_consts = _ret_consts()
    bf = lambda w: w.astype(BF16)
    ffn1_wi, ffn1_wo, ffn2_wi, ffn2_wo, mix_wi, mix_wo = map(bf, (ffn1_w_in, ffn1_w_out, ffn2_w_in, ffn2_w_out,
                                                                  w_in, w_out))
    gains = lambda g: g[:, None, :]
    g_ffn1, g_mix, g_ffn2 = gains(norm_ffn1_g), gains(norm_mix_g), gains(norm_ffn2_g)
    for l in range(DEPTH):
        xf = _ffn_call(xf, mod, g_ffn1, ffn1_wi, ffn1_wo, l, mod_row=0)

        rx, rg, qt, k, vt, tq, tk, tv, tg = _proj_call(xf, mod, g_mix, mix_wi, tables, l)
        y_rec = _rglru_call(rx, rg, rec_conv_w[l], rec_conv_b[l],
                            _block_diag(rec_gate_a_w[l]).astype(BF16), rec_gate_a_b[l],
                            _block_diag(rec_gate_x_w[l]).astype(BF16), rec_gate_x_b[l], rec_lambda[l])
        lambda_init = 0.8 - 0.6 * math.exp(-0.3 * l)
        lam_vecs = jnp.stack([diff_lambda_q1[l], diff_lambda_k1[l], diff_lambda_q2[l], diff_lambda_k2[l]])
        lam_vecs = jnp.pad(lam_vecs, ((0, 0), (0, LANES - DIFF_SUB)))
        y_diff = _diff_call(lam_vecs, qt, k, vt, diff_subln_g[l], lambda_init)
        y_ret = _ret_call(tq, tk, tv, tg, ret_consts)

        xf = _ffn_call(xf, mod, g_ffn2, ffn2_wi, ffn2_wo, l, mod_row=6, mix=(y_rec, y_diff, y_ret, mix_wo),
                       final_g=final_norm_g if l == DEPTH - 1 else None)
    return xf.reshape(BATCH, SEQ, D_MODEL)
```

```python
import functools
import math

import jax
import jax.numpy as jnp
from jax import lax
from jax.experimental import pallas as pl
from jax.experimental.pallas import tpu as pltpu

F32 = jnp.float32
BF16 = jnp.bfloat16

D_MODEL = 1024
BATCH = 4
SEQ = 4096
DEPTH = 4
N_TOK = BATCH * SEQ
HEAD_DIM = 64
REC_WIDTH = 384
REC_HEADS = 6
REC_CONV = 4
REC_C = 8.0
DIFF_WIDTH = 384
DIFF_HEADS = 6
DIFF_SUB = 32
ROPE_THETA = 500000.0
ROPE_DIM = 8
RET_WIDTH = 256
RET_HEADS = 4
RET_ROT_BASE = 10000.0
IN_WIDTH = 2944
D_FF = 2816
N_MOD = 9
EPS = 1e-6

LANES = 128
SUBLANES = 8
MXU_DIM = 256
VMEM_LIMIT_BYTES = 56 * 1024 * 1024

TM = 1024
TM_FFN = 1024
FF_CHUNK = MXU_DIM
N_FF_CHUNKS = D_FF // FF_CHUNK
ATT_BLK = 256
ATT_ACC_ROWS = HEAD_DIM + 16
ATT_UNROLL = 4
ATT_GROUPS = 1
RET_BLK = 256
RET_TILE = 2048
LRU_BLK = 1024
TAB_BLK = 1024
NEG_BIG = -1e30


def _sigmoid(x):
    return 1.0 / (1.0 + jnp.exp(-x))


def _split_bf16(x):
    hi = x.astype(BF16)
    return hi, (x - hi.astype(F32)).astype(BF16)


def _norm_mod(x, g, shift, scale):
    ms = jnp.mean(x * x, axis=-1, keepdims=True)
    y = x * lax.rsqrt(ms + EPS) * g
    return y * (1.0 + scale) + shift


def _resident_layer(stacked_shape, layer):
    nd = len(stacked_shape) - 1
    return pl.BlockSpec((None,) + tuple(stacked_shape[1:]), lambda *_: (layer,) + (0,) * nd,
                        pipeline_mode=pl.Buffered(1))


def _mod_spec(layer, rows_per_step):
    steps_per_batch = SEQ // rows_per_step
    return pl.BlockSpec((None, 1, N_MOD, D_MODEL), lambda i: (layer, i // steps_per_batch, 0, 0))


def _gain_spec(layer):
    return pl.BlockSpec((None, 1, D_MODEL), lambda i: (layer, 0, 0))


def _params(n_axes, vmem=VMEM_LIMIT_BYTES, flags=None):
    return pltpu.CompilerParams(dimension_semantics=("arbitrary",) * n_axes, vmem_limit_bytes=vmem, flags=flags)


def _ada_kernel(c_ref, w_ref, b_ref, o_ref):
    c = c_ref[...]
    s = c * _sigmoid(c)
    s_hi, s_lo = _split_bf16(s)
    w_hi, w_lo = _split_bf16(w_ref[0])
    dot = functools.partial(jnp.dot, preferred_element_type=F32)
    o_ref[0] = dot(s_hi, w_hi) + (dot(s_hi, w_lo) + dot(s_lo, w_hi)) + b_ref[0]


def _ada_call(c, ada_w, ada_b):
    tn = 1536
    nmod = N_MOD * D_MODEL
    return pl.pallas_call(
        _ada_kernel,
        grid=(DEPTH, nmod // tn),
        in_specs=[
            pl.BlockSpec((BATCH, D_MODEL), lambda l, j: (0, 0)),
            pl.BlockSpec((1, D_MODEL, tn), lambda l, j: (l, 0, j)),
            pl.BlockSpec((1, 1, tn), lambda l, j: (l, 0, j)),
        ],
        out_specs=pl.BlockSpec((1, BATCH, tn), lambda l, j: (l, 0, j)),
        out_shape=jax.ShapeDtypeStruct((DEPTH, BATCH, nmod), F32),
        compiler_params=_params(2),
        name="ada_mod",
    )(c, ada_w, ada_b.reshape(DEPTH, 1, nmod))


def _tables_kernel(pos_ref, inv_ref, sel_ref, one_ref, cd_ref, sd_ref, cr_ref, sr_ref):
    ang = pos_ref[...].astype(F32) * inv_ref[...]
    dot = functools.partial(jnp.dot, preferred_element_type=F32)
    spread = lambda parts, m: dot(parts[0], sel_ref[m]) + dot(parts[1], sel_ref[m])
    cos, sin = _split_bf16(jnp.cos(ang)), _split_bf16(jnp.sin(ang))
    cd_ref[...] = spread(cos, 0) + one_ref[...]
    sd_ref[...] = spread(sin, 1)
    cr_ref[...] = spread(cos, 2)
    sr_ref[...] = spread(sin, 3)


def _rotary_consts():
    lane = jnp.arange(LANES)
    half_d, half_r = ROPE_DIM // 2, HEAD_DIM // 2
    inv_d = jnp.power(jnp.float32(ROPE_THETA), -jnp.arange(half_d, dtype=F32) * (2.0 / ROPE_DIM))
    inv_r = jnp.power(jnp.float32(RET_ROT_BASE), -jnp.arange(half_r, dtype=F32) * (2.0 / HEAD_DIM))
    inv = jnp.zeros((LANES,), F32).at[:half_r].set(inv_r).at[half_r:half_r + half_d].set(inv_d)
    cd = lane % DIFF_SUB
    src_d = half_r + cd % half_d
    on_d = cd < ROPE_DIM
    sgn_d = jnp.where(cd < half_d, -1.0, 1.0)
    cr = lane % HEAD_DIM
    src_r = cr % half_r
    sgn_r = jnp.where(cr < half_r, -1.0, 1.0)
    pick = lambda src, weight: jnp.where(lane[:, None] == src[None, :], weight[None, :], 0.0)
    ones = jnp.ones((LANES,), F32)
    sel = jnp.stack([pick(src_d, jnp.where(on_d, 1.0, 0.0)), pick(src_d, jnp.where(on_d, sgn_d, 0.0)),
                     pick(src_r, ones), pick(src_r, sgn_r)]).astype(BF16)
    return inv.reshape(1, LANES), sel, jnp.where(on_d, 0.0, 1.0).astype(F32).reshape(1, LANES)


def _tables_call(positions):
    inv, sel, one_d = _rotary_consts()
    vec = pl.BlockSpec((1, LANES), lambda i: (0, 0))
    tab = pl.BlockSpec((TAB_BLK, LANES), lambda i: (i, 0))
    return pl.pallas_call(
        _tables_kernel,
        grid=(N_TOK // TAB_BLK,),
        in_specs=[pl.BlockSpec((TAB_BLK, 1), lambda i: (i, 0)), vec,
                  pl.BlockSpec(sel.shape, lambda i: (0, 0, 0)), vec],
        out_specs=[tab, tab, tab, tab],
        out_shape=[jax.ShapeDtypeStruct((N_TOK, LANES), F32)] * 4,
        compiler_params=_params(1),
        name="rotary_tables",
    )(positions.reshape(N_TOK, 1), inv, sel, one_d)


def _ffn_kernel(*refs, mod_row, with_mix, with_final):
    refs = list(refs)
    x_ref, mod_ref, g_ref = refs[:3]
    del refs[:3]
    if with_mix:
        yr_ref, yd_ref, yt_ref, wm_ref = refs[:4]
        del refs[:4]
    wi_ref, wo_ref = refs[:2]
    del refs[:2]
    if with_final:
        gf_ref = refs.pop(0)
    o_ref, acc_ref = refs

    x = x_ref[...]
    if with_mix:
        y_cat = jnp.concatenate([yr_ref[...], yd_ref[...], yt_ref[...]], axis=1)
        y = jnp.dot(y_cat, wm_ref[...], preferred_element_type=F32)
        x = x + mod_ref[0, mod_row - 1:mod_row, :] * y
    shift = mod_ref[0, mod_row:mod_row + 1, :]
    scale = mod_ref[0, mod_row + 1:mod_row + 2, :]
    gate = mod_ref[0, mod_row + 2:mod_row + 3, :]
    h = _norm_mod(x, g_ref[...], shift, scale).astype(BF16)
    for c in range(N_FF_CHUNKS):
        lo = c * FF_CHUNK
        hg = jnp.dot(h, wi_ref[:, lo:lo + FF_CHUNK], preferred_element_type=F32)
        hu = jnp.dot(h, wi_ref[:, D_FF + lo:D_FF + lo + FF_CHUNK], preferred_element_type=F32)
        acc_ref[:, lo:lo + FF_CHUNK] = (hg * _sigmoid(hg) * hu).astype(BF16)
    out = x + (0.5 * gate) * jnp.dot(acc_ref[...], wo_ref[...], preferred_element_type=F32)
    if with_final:
        ms = jnp.mean(out * out, axis=-1, keepdims=True)
        out = out * lax.rsqrt(ms + EPS) * gf_ref[...]
    o_ref[...] = out


def _ffn_call(x, mod, gains, w_in, w_out, layer, mod_row, mix=None, final_g=None):
    tok = lambda w_: pl.BlockSpec((TM_FFN, w_), lambda i: (i, 0))
    operands = [x, mod, gains]
    in_specs = [tok(D_MODEL), _mod_spec(layer, TM_FFN), _gain_spec(layer)]
    if mix is not None:
        operands += list(mix)
        in_specs += [tok(REC_WIDTH), tok(DIFF_WIDTH), tok(RET_WIDTH), _resident_layer(mix[3].shape, layer)]
    operands += [w_in, w_out]
    in_specs += [_resident_layer(w_in.shape, layer), _resident_layer(w_out.shape, layer)]
    if final_g is not None:
        operands.append(final_g.reshape(1, D_MODEL))
        in_specs.append(pl.BlockSpec((1, D_MODEL), lambda i: (0, 0)))
    return pl.pallas_call(
        functools.partial(_ffn_kernel, mod_row=mod_row, with_mix=mix is not None, with_final=final_g is not None),
        grid=(N_TOK // TM_FFN,),
        in_specs=in_specs,
        out_specs=tok(D_MODEL),
        out_shape=jax.ShapeDtypeStruct((N_TOK, D_MODEL), F32),
        scratch_shapes=[pltpu.VMEM((TM_FFN, D_FF), BF16)],
        compiler_params=_params(1),
        name="ffn_swiglu",
    )(*operands)


def _rope(x, cos, sin, lo_mask, shift_up, shift_dn):
    outs = []
    for j in range(x.shape[1] // LANES):
        xj = x[:, j * LANES:(j + 1) * LANES]
        partner = jnp.where(lo_mask, pltpu.roll(xj, shift_up, 1), pltpu.roll(xj, shift_dn, 1))
        outs.append(xj * cos + partner * sin)
    return jnp.concatenate(outs, axis=1)


def _proj_kernel(x_ref, mod_ref, g_ref, w_ref, cd_ref, sd_ref, cr_ref, sr_ref,
                 rx_ref, rg_ref, qt_ref, k_ref, vt_ref, tq_ref, tk_ref, tv_ref, tg_ref):
    x = x_ref[...]
    shift = mod_ref[0, 3:4, :]
    scale = mod_ref[0, 4:5, :]
    h = _norm_mod(x, g_ref[...], shift, scale).astype(BF16)

    groups = {}
    for lo, hi in ((0, 768), (768, 1536), (1536, IN_WIDTH)):
        groups[lo] = jnp.dot(h, w_ref[:, lo:hi], preferred_element_type=F32)

    def proj(lo, width):
        base = max(b for b in groups if b <= lo)
        return groups[base][:, lo - base:lo - base + width]

    lane = lax.broadcasted_iota(jnp.int32, (TM, LANES), 1)
    lo_d = (lane & (DIFF_SUB - 1)) < (ROPE_DIM // 2)
    lo_r = (lane & (HEAD_DIM - 1)) < (HEAD_DIM // 2)
    cd, sd, cr, sr = cd_ref[...], sd_ref[...], cr_ref[...], sr_ref[...]
    rope_d = lambda v: _rope(v, cd, sd, lo_d, LANES - ROPE_DIM // 2, ROPE_DIM // 2)
    rope_r = lambda v: _rope(v, cr, sr, lo_r, LANES - HEAD_DIM // 2, HEAD_DIM // 2)

    rx_ref[...] = proj(0, REC_WIDTH)
    rg_ref[...] = proj(384, REC_WIDTH)
    q = rope_d(proj(768, DIFF_WIDTH)) * (DIFF_SUB ** -0.5 * math.log2(math.e))
    qt = q.T
    k_ref[...] = rope_d(proj(1152, DIFF_WIDTH)).astype(BF16)
    vt = proj(1536, DIFF_WIDTH).T
    for j in range(TM // ATT_BLK):
        qt_ref[0, j] = qt[:, j * ATT_BLK:(j + 1) * ATT_BLK].astype(BF16)
        vt_ref[0, j] = vt[:, j * ATT_BLK:(j + 1) * ATT_BLK].astype(BF16)
    tq_ref[...] = rope_r(proj(1920, RET_WIDTH)).astype(BF16)
    tk_ref[...] = (rope_r(proj(2176, RET_WIDTH)) * (HEAD_DIM ** -0.5)).astype(BF16)
    tv_ref[...] = proj(2432, RET_WIDTH).astype(BF16)
    tg_ref[...] = proj(2688, RET_WIDTH)


def _proj_call(x, mod, gains, w, tables, layer):
    tiles_per_batch = SEQ // TM
    blocks_per_tile = TM // ATT_BLK
    n_blk = SEQ // ATT_BLK
    tok = lambda w_, : pl.BlockSpec((TM, w_), lambda i: (i, 0))
    tspec = pl.BlockSpec((1, blocks_per_tile, DIFF_WIDTH, ATT_BLK),
                         lambda i: (i // tiles_per_batch, i % tiles_per_batch, 0, 0))
    tshape = jax.ShapeDtypeStruct((BATCH, n_blk, DIFF_WIDTH, ATT_BLK), BF16)
    tok_shape = lambda w_, dt: jax.ShapeDtypeStruct((N_TOK, w_), dt)
    return pl.pallas_call(
        _proj_kernel,
        grid=(N_TOK // TM,),
        in_specs=[
            tok(D_MODEL), _mod_spec(layer, TM), _gain_spec(layer), _resident_layer(w.shape, layer),
            tok(LANES), tok(LANES), tok(LANES), tok(LANES),
        ],
        out_specs=[tok(REC_WIDTH), tok(REC_WIDTH), tspec, tok(DIFF_WIDTH), tspec,
                   tok(RET_WIDTH), tok(RET_WIDTH), tok(RET_WIDTH), tok(RET_WIDTH)],
        out_shape=[tok_shape(REC_WIDTH, F32), tok_shape(REC_WIDTH, F32), tshape, tok_shape(DIFF_WIDTH, BF16), tshape,
                   tok_shape(RET_WIDTH, BF16), tok_shape(RET_WIDTH, BF16), tok_shape(RET_WIDTH, BF16),
                   tok_shape(RET_WIDTH, F32)],
        compiler_params=_params(1),
        name="mixer_in_proj",
    )(x, mod, gains, w, *tables)


def _rglru_kernel(rx_ref, rg_ref, cw_ref, cb_ref, wa_ref, ba_ref, wx_ref, bx_ref, lam_ref,
                  y_ref, xe_ref, h_ref):
    @pl.when(pl.program_id(1) == 0)
    def _():
        xe_ref[...] = jnp.zeros_like(xe_ref)
        h_ref[...] = jnp.zeros_like(h_ref)

    n_grp = LRU_BLK // SUBLANES
    x = rx_ref[...]
    x3 = x.reshape(n_grp, SUBLANES, REC_WIDTH)
    tail = xe_ref[...]
    within = lax.broadcasted_iota(jnp.int32, (n_grp, SUBLANES, REC_WIDTH), 1)
    cw = cw_ref[...]
    xc3 = cb_ref[...] + cw[REC_CONV - 1:REC_CONV, :] * x3
    for k in range(1, REC_CONV):
        rot = pltpu.roll(x3, k, 1)
        before = jnp.concatenate([pltpu.roll(tail, k, 0)[None], rot[:-1]], axis=0)
        xc3 = xc3 + cw[REC_CONV - 1 - k:REC_CONV - k, :] * jnp.where(within >= k, rot, before)
    xe_ref[...] = x[LRU_BLK - SUBLANES:, :]
    xc = xc3.reshape(LRU_BLK, REC_WIDTH)

    xb = xc.astype(BF16)
    r = _sigmoid(jnp.dot(xb, wa_ref[...], preferred_element_type=F32) + ba_ref[...])
    gi = _sigmoid(jnp.dot(xb, wx_ref[...], preferred_element_type=F32) + bx_ref[...])
    nl = -lam_ref[...]
    softplus = jnp.maximum(nl, 0.0) + jnp.log1p(jnp.exp(-jnp.abs(nl)))
    log_a = -REC_C * r * softplus
    a = jnp.exp(log_a)
    one_minus_a2 = -jnp.tanh(log_a) * (a * a + 1.0)
    root = jnp.where(one_minus_a2 > 0.0, one_minus_a2 * lax.rsqrt(one_minus_a2), 0.0)
    b = root * (gi * xc)

    a = a.reshape(n_grp, SUBLANES, REC_WIDTH)
    b = b.reshape(n_grp, SUBLANES, REC_WIDTH)
    s = 1
    while s < SUBLANES:
        valid = within >= s
        a_prev = pltpu.roll(a, s, 1)
        b_prev = pltpu.roll(b, s, 1)
        b = jnp.where(valid, a * b_prev, 0.0) + b
        a = jnp.where(valid, a * a_prev, a)
        s *= 2
    carry = h_ref[...]
    groups = []
    for grp in range(n_grp):
        hg = a[grp] * carry + b[grp]
        groups.append(hg)
        carry = hg[SUBLANES - 1:, :]
    h = jnp.concatenate(groups, axis=0)
    h_ref[...] = carry

    gate = rg_ref[...]
    cdf = 0.5 * (1.0 + jnp.tanh(math.sqrt(2.0 / math.pi) * (gate + 0.044715 * (gate * gate * gate))))
    y_ref[...] = (gate * cdf * h).astype(BF16)


def _rglru_call(rx, rg, conv_w, conv_b, wa_bd, ba, wx_bd, bx, lam):
    nt = SEQ // LRU_BLK
    tok = pl.BlockSpec((LRU_BLK, REC_WIDTH), lambda b, t: (b * nt + t, 0))
    vec = pl.BlockSpec((1, REC_WIDTH), lambda b, t: (0, 0))
    mat = pl.BlockSpec((REC_WIDTH, REC_WIDTH), lambda b, t: (0, 0))
    row = lambda v: v.reshape(1, REC_WIDTH)
    return pl.pallas_call(
        _rglru_kernel,
        grid=(BATCH, nt),
        in_specs=[tok, tok, pl.BlockSpec((REC_CONV, REC_WIDTH), lambda b, t: (0, 0)), vec, mat, vec, mat, vec, vec],
        out_specs=tok,
        out_shape=jax.ShapeDtypeStruct((N_TOK, REC_WIDTH), BF16),
        scratch_shapes=[pltpu.VMEM((SUBLANES, REC_WIDTH), F32), pltpu.VMEM((1, REC_WIDTH), F32)],
        compiler_params=_params(2),
        name="rglru",
    )(rx, rg, conv_w, row(conv_b), wa_bd, row(ba), wx_bd, row(bx), row(lam))


def _block_diag(w):
    h, d, _ = w.shape
    eye = jnp.eye(h, dtype=w.dtype)
    return jnp.einsum('hij,hk->hikj', w, eye).reshape(h * d, h * d)


def _diff_kernel(lam_ref, qt_ref, k_ref, vt_ref, g_ref, o_ref,
                 qm_ref, st_ref, mb_ref, p_ref, al_ref, m_ref, acc_ref, *, lambda_init):
    n_grp = qt_ref.shape[2] // LANES
    n_sc = 4 * n_grp
    lv = lam_ref[...]
    lam = (jnp.exp(jnp.sum(lv[0:1] * lv[1:2], axis=-1, keepdims=True))
           - jnp.exp(jnp.sum(lv[2:3] * lv[3:4], axis=-1, keepdims=True)) + lambda_init)

    def query_block(i, carry):
        _diff_query_block(i, lam, qt_ref, k_ref, vt_ref, g_ref, o_ref, qm_ref, st_ref, mb_ref, p_ref, al_ref,
                          m_ref, acc_ref, n_grp=n_grp, n_sc=n_sc, lambda_init=lambda_init)
        return carry

    lax.fori_loop(0, SEQ // ATT_BLK, query_block, 0)


def _diff_query_block(i, lam, qt_ref, k_ref, vt_ref, g_ref, o_ref, qm_ref, st_ref, mb_ref, p_ref, al_ref,
                      m_ref, acc_ref, *, n_grp, n_sc, lambda_init):
    rowq = lax.broadcasted_iota(jnp.int32, (LANES, ATT_BLK), 0)
    for idx in range(n_sc):
        grp, sub = divmod(idx, 4)
        qt = qt_ref[0, i, grp * LANES:(grp + 1) * LANES, :]
        band = (rowq >= sub * DIFF_SUB) & (rowq < (sub + 1) * DIFF_SUB)
        qm_ref[idx] = jnp.where(band, qt, jnp.zeros_like(qt))
        m_ref[idx] = jnp.full((1, ATT_BLK), NEG_BIG, F32)
        acc_ref[idx] = jnp.zeros((ATT_ACC_ROWS, ATT_BLK), F32)
    ones_rows = jnp.ones((ATT_ACC_ROWS - HEAD_DIM, ATT_BLK), BF16)

    def scores(t, slot, masked):
        rows = pl.ds(pl.multiple_of(t * ATT_BLK, ATT_BLK), ATT_BLK)
        if masked:
            kk = lax.broadcasted_iota(jnp.int32, (ATT_BLK, ATT_BLK), 0)
            qq = lax.broadcasted_iota(jnp.int32, (ATT_BLK, ATT_BLK), 1)
            causal = kk <= qq
        for grp in range(n_grp):
            kb = k_ref[0, rows, grp * LANES:(grp + 1) * LANES]
            for idx in range(4 * grp, 4 * grp + 4):
                st = jnp.dot(kb, qm_ref[idx], preferred_element_type=F32)
                if masked:
                    st = jnp.where(causal, st, NEG_BIG)
                st_ref[slot, idx] = st
                mb_ref[slot, idx] = jnp.max(st, axis=0, keepdims=True)

    def softmax(slot):
        for idx in range(n_sc):
            m_old = m_ref[idx]
            m_new = jnp.maximum(m_old, mb_ref[slot, idx])
            p_ref[slot, idx] = jnp.exp2(st_ref[slot, idx] - m_new).astype(BF16)
            al_ref[slot, idx] = jnp.exp2(m_old - m_new)
            m_ref[idx] = m_new

    def values(t, slot):
        for head in range(2 * n_grp):
            vh = jnp.concatenate([vt_ref[0, t, head * HEAD_DIM:(head + 1) * HEAD_DIM, :], ones_rows], axis=0)
            for idx in (2 * head, 2 * head + 1):
                acc_ref[idx] = (al_ref[slot, idx] * acc_ref[idx]
                                + jnp.dot(vh, p_ref[slot, idx], preferred_element_type=F32))

    def advance(t, slot, masked, in_flight=2):
        if in_flight >= 2:
            values(t - 2, slot)
        if in_flight >= 1:
            softmax(1 - slot)
        scores(t, slot, masked)

    def drain(last, last_slot, in_flight=2):
        if in_flight >= 2:
            values(last - 1, 1 - last_slot)
        softmax(last_slot)
        values(last, last_slot)

    def group(u, carry):
        for j in range(ATT_UNROLL):
            advance(ATT_UNROLL * u + j, j % 2, False)
        return carry

    full_groups = i // ATT_UNROLL

    @pl.when(full_groups > 0)
    def _():
        for j in range(ATT_UNROLL):
            advance(j, j % 2, False, in_flight=j)

    lax.fori_loop(1, full_groups, group, 0)

    for rem in range(ATT_UNROLL):
        @pl.when((i % ATT_UNROLL == rem) & (full_groups > 0))
        def _(rem=rem):
            for j in range(rem):
                advance(i - rem + j, j % 2, False)
            advance(i, rem % 2, True)
            drain(i, rem % 2)

        @pl.when(i == rem)
        def _(rem=rem):
            for j in range(rem):
                advance(j, j % 2, False, in_flight=j)
            advance(rem, rem % 2, True, in_flight=rem)
            drain(rem, rem % 2, in_flight=rem + 1)

    out_rows = pl.ds(pl.multiple_of(i * ATT_BLK, ATT_BLK), ATT_BLK)
    for grp in range(n_grp):
        outs = []
        for head in (2 * grp, 2 * grp + 1):
            a0, a1 = acc_ref[2 * head], acc_ref[2 * head + 1]
            o = (a0[:HEAD_DIM] / a0[HEAD_DIM:HEAD_DIM + 1]
                 - lam * (a1[:HEAD_DIM] / a1[HEAD_DIM:HEAD_DIM + 1]))
            ms = jnp.mean(o * o, axis=0, keepdims=True)
            outs.append(o * lax.rsqrt(ms + EPS) * g_ref[...] * (1.0 - lambda_init))
        o_ref[out_rows, grp * LANES:(grp + 1) * LANES] = jnp.concatenate(outs, axis=0).T.astype(BF16)


def _diff_call(lam_vecs, qt, k, vt, subln_g, lambda_init):
    nq = SEQ // ATT_BLK
    width = ATT_GROUPS * LANES
    n_sc = 4 * ATT_GROUPS
    return pl.pallas_call(
        functools.partial(_diff_kernel, lambda_init=lambda_init),
        grid=(BATCH, DIFF_WIDTH // width),
        in_specs=[
            pl.BlockSpec((4, LANES), lambda b, g: (0, 0)),
            pl.BlockSpec((1, nq, width, ATT_BLK), lambda b, g: (b, 0, g, 0)),
            pl.BlockSpec((1, SEQ, width), lambda b, g: (b, 0, g)),
            pl.BlockSpec((1, nq, width, ATT_BLK), lambda b, g: (b, 0, g, 0)),
            pl.BlockSpec((HEAD_DIM, 1), lambda b, g: (0, 0)),
        ],
        out_specs=pl.BlockSpec((SEQ, width), lambda b, g: (b, g)),
        out_shape=jax.ShapeDtypeStruct((N_TOK, DIFF_WIDTH), BF16),
        scratch_shapes=[
            pltpu.VMEM((n_sc, LANES, ATT_BLK), BF16),
            pltpu.VMEM((2, n_sc, ATT_BLK, ATT_BLK), F32),
            pltpu.VMEM((2, n_sc, 1, ATT_BLK), F32),
            pltpu.VMEM((2, n_sc, ATT_BLK, ATT_BLK), BF16),
            pltpu.VMEM((2, n_sc, 1, ATT_BLK), F32),
            pltpu.VMEM((n_sc, 1, ATT_BLK), F32),
            pltpu.VMEM((n_sc, ATT_ACC_ROWS, ATT_BLK), F32),
        ],
        compiler_params=_params(2),
        name="diff_attn",
    )(lam_vecs, qt, k.reshape(BATCH, SEQ, DIFF_WIDTH), vt, subln_g.reshape(HEAD_DIM, 1))


def _ret_kernel(q_ref, k_ref, v_ref, g_ref, dec_ref, kdec_ref, qdec_ref, cdec_ref, bmask_ref, o_ref, s_ref):
    @pl.when(pl.program_id(1) == 0)
    def _():
        s_ref[...] = jnp.zeros_like(s_ref)

    lane = lax.broadcasted_iota(jnp.int32, (RET_BLK, LANES), 1)
    first = lane < HEAD_DIM
    for pair in range(RET_HEADS // 2):
        lanes = slice(pair * LANES, (pair + 1) * LANES)
        state = s_ref[pair]
        for chunk in range(RET_TILE // RET_BLK):
            rows = slice(chunk * RET_BLK, (chunk + 1) * RET_BLK)
            q = q_ref[rows, lanes]
            k = k_ref[rows, lanes]
            v = v_ref[rows, lanes]
            o = jnp.dot(q, state.astype(BF16), preferred_element_type=F32) * qdec_ref[pair]
            for hh in range(2):
                sel = first if hh == 0 else jnp.logical_not(first)
                qm = jnp.where(sel, q, jnp.zeros_like(q))
                sc = lax.dot_general(qm, k, (((1,), (1,)), ((), ())),
                                     preferred_element_type=F32) * dec_ref[pair, hh]
                oi = jnp.dot(sc.astype(BF16), v, preferred_element_type=F32)
                o = o + jnp.where(sel, oi, 0.0)
            kd_t = (k.astype(F32) * kdec_ref[pair]).T.astype(BF16)
            kv = jnp.dot(kd_t, v, preferred_element_type=F32)
            state = state * cdec_ref[pair] + kv * bmask_ref[...]

            o2 = o * o
            ms0 = jnp.sum(jnp.where(first, o2, 0.0), axis=-1, keepdims=True) * (1.0 / HEAD_DIM)
            ms1 = jnp.sum(jnp.where(first, 0.0, o2), axis=-1, keepdims=True) * (1.0 / HEAD_DIM)
            rs = jnp.where(first, lax.rsqrt(ms0 + EPS), lax.rsqrt(ms1 + EPS))
            gate = g_ref[rows, lanes]
            o_ref[rows, lanes] = (gate * _sigmoid(gate) * (o * rs)).astype(BF16)
        s_ref[pair] = state


def _ret_consts():
    c = RET_BLK
    lg = jnp.log1p(-jnp.exp2(-5.0 - jnp.arange(RET_HEADS, dtype=F32)))
    idx = jnp.arange(c, dtype=F32)
    rel = idx[:, None] - idx[None, :]
    decay = jnp.where(rel >= 0, jnp.exp(jnp.maximum(rel, 0.0)[None] * lg[:, None, None]), 0.0)
    k_decay = jnp.exp((c - 1.0 - idx)[None, :] * lg[:, None])
    q_decay = jnp.exp((idx + 1.0)[None, :] * lg[:, None])
    chunk_decay = jnp.exp(c * lg)
    pairs = RET_HEADS // 2
    lanes = lambda t: jnp.repeat(t.reshape(pairs, 2, c).transpose(0, 2, 1), HEAD_DIM, axis=2)
    cdec = jnp.broadcast_to(jnp.repeat(chunk_decay.reshape(pairs, 2), HEAD_DIM, axis=1)[:, :, None],
                            (pairs, LANES, LANES))
    blk = jnp.arange(LANES) // HEAD_DIM
    bmask = (blk[:, None] == blk[None, :]).astype(F32)
    return decay.reshape(pairs, 2, c, c), lanes(k_decay), lanes(q_decay), cdec, bmask


def _ret_call(tq, tk, tv, tg, consts):
    decay, kdec, qdec, cdec, bmask = consts
    nt = SEQ // RET_TILE
    tok = pl.BlockSpec((RET_TILE, RET_WIDTH), lambda b, t: (b * nt + t, 0))
    whole = lambda a: pl.BlockSpec(a.shape, lambda b, t: (0,) * a.ndim)
    return pl.pallas_call(
        _ret_kernel,
        grid=(BATCH, nt),
        in_specs=[tok, tok, tok, tok, whole(decay), whole(kdec), whole(qdec), whole(cdec), whole(bmask)],
        out_specs=tok,
        out_shape=jax.ShapeDtypeStruct((N_TOK, RET_WIDTH), BF16),
        scratch_shapes=[pltpu.VMEM((RET_HEADS // 2, LANES, LANES), F32)],
        compiler_params=_params(2),
        name="retention",
    )(tq, tk, tv, tg, decay, kdec, qdec, cdec, bmask)


def kernel(x, c, positions, norm_ffn1_g, norm_mix_g, norm_ffn2_g, ada_w, ada_b, ffn1_w_in, ffn1_w_out, ffn2_w_in, ffn2_w_out, w_in, w_out, rec_conv_w, rec_conv_b, rec_gate_a_w, rec_gate_a_b, rec_gate_x_w, rec_gate_x_b, rec_lambda, diff_lambda_q1, diff_lambda_k1, diff_lambda_q2, diff_lambda_k2, diff_subln_g, final_norm_g):
    xf = x.reshape(N_TOK, D_MODEL)
    mod = _ada_call(c, ada_w, ada_b).reshape(DEPTH, BATCH, N_MOD, D_MODEL)
    tables = _tables_call(positions)
    ret_consts = _ret_consts()
    bf = lambda w: w.astype(BF16)
    ffn1_wi, ffn1_wo, ffn2_wi, ffn2_wo, mix_wi, mix_wo = map(bf, (ffn1_w_in, ffn1_w_out, ffn2_w_in, ffn2_w_out,
                                                                  w_in, w_out))
    gains = lambda g: g[:, None, :]
    g_ffn1, g_mix, g_ffn2 = gains(norm_ffn1_g), gains(norm_mix_g), gains(norm_ffn2_g)
    for l in range(DEPTH):
        xf = _ffn_call(xf, mod, g_ffn1, ffn1_wi, ffn1_wo, l, mod_row=0)

        rx, rg, qt, k, vt, tq, tk, tv, tg = _proj_call(xf, mod, g_mix, mix_wi, tables, l)
        y_rec = _rglru_call(rx, rg, rec_conv_w[l], rec_conv_b[l],
                            _block_diag(rec_gate_a_w[l]).astype(BF16), rec_gate_a_b[l],
                            _block_diag(rec_gate_x_w[l]).astype(BF16), rec_gate_x_b[l], rec_lambda[l])
        lambda_init = 0.8 - 0.6 * math.exp(-0.3 * l)
        lam_vecs = jnp.stack([diff_lambda_q1[l], diff_lambda_k1[l], diff_lambda_q2[l], diff_lambda_k2[l]])
        lam_vecs = jnp.pad(lam_vecs, ((0, 0), (0, LANES - DIFF_SUB)))
        y_diff = _diff_call(lam_vecs, qt, k, vt, diff_subln_g[l], lambda_init)
        y_ret = _ret_call(tq, tk, tv, tg, ret_consts)

        xf = _ffn_call(xf, mod, g_ffn2, ffn2_wi, ffn2_wo, l, mod_row=6, mix=(y_rec, y_diff, y_ret, mix_wo),
                       final_g=final_norm_g if l == DEPTH - 1 else None)
    return xf.reshape(BATCH, SEQ, D_MODEL)
```
